```python
import jax, jax.numpy as jnp
from jax import lax
import numpy as np

D_MODEL = 2048
BATCH = 4
SEQ = 2048
DEPTH = 2

GRID_W = 64
CTX_LEN = 256
RET_HEADS = 8
RET_HEAD_DIM = 128
RET_W = RET_HEADS * RET_HEAD_DIM
K_SCALE = RET_HEAD_DIM ** -0.5
CHUNK = 128
ROPE_BASE = 10000.0
CONV_GROUPS = 8
CONV_W = D_MODEL - RET_W
CONV_WIDTH = 3
IN_W = 4 * RET_W + 3 * CONV_W
POOL_WINDOWS = (2, 4, 8, 16)
POOL_GROUPS = len(POOL_WINDOWS)
POOL_GROUP_W = D_MODEL // POOL_GROUPS
D_FF = 5632
MACARON = 0.5
N_MOD = 9
N_EVEN = (DEPTH + 1) // 2
N_ODD = DEPTH // 2
EPS = 1e-6
GN_EPS = 1e-5

kernel_name = "hybrid_retention_shortconv_pool_macaron_dit"


def rmsnorm(x, g):
    xf = x.astype(jnp.float32)
    n = xf * lax.rsqrt(jnp.mean(xf * xf, axis=-1, keepdims=True) + EPS)
    return (n * g.astype(jnp.float32)).astype(x.dtype)


def modulate(h, shift, scale):
    return h * (1.0 + scale) + shift


def adaln(cond, w_mod_l, b_mod_l):
    m = jax.nn.silu(cond) @ w_mod_l + b_mod_l
    return jnp.split(m[..., None, :], N_MOD, axis=-1)


def ffn_sublayer(x, g_norm, shift, scale, gate, w_gate, w_up, w_down):
    h = modulate(rmsnorm(x, g_norm), shift, scale)
    y = (jax.nn.silu(h @ w_gate) * (h @ w_up)) @ w_down
    return x + MACARON * gate * y


def to_heads(t):
    b, l, _ = t.shape
    return t.reshape(b, l, RET_HEADS, RET_HEAD_DIM).transpose(0, 2, 1, 3).astype(jnp.float32)


def axial_rope_tables(rows, cols):
    quarter = RET_HEAD_DIM // 4
    inv = ROPE_BASE ** (-jnp.arange(quarter, dtype=jnp.float32) / quarter)
    ang = jnp.concatenate([rows.astype(jnp.float32)[:, None] * inv,
                           cols.astype(jnp.float32)[:, None] * inv], axis=-1)
    return jnp.cos(ang), jnp.sin(ang)


def apply_rope(t, cos, sin):
    half = RET_HEAD_DIM // 2
    t1, t2 = t[..., :half], t[..., half:]
    return jnp.concatenate([t1 * cos - t2 * sin, t1 * sin + t2 * cos], axis=-1)


def retention_chunked(q, k, v, log_gamma, s0):
    b, h, l, dk = q.shape
    dv = v.shape[-1]
    n = l // CHUNK
    lg = log_gamma.astype(jnp.float32)
    qc = q.reshape(b, h, n, CHUNK, dk)
    kc = k.reshape(b, h, n, CHUNK, dk)
    vc = v.reshape(b, h, n, CHUNK, dv)
    idx = jnp.arange(CHUNK, dtype=jnp.float32)
    diff = idx[:, None] - idx[None, :]
    decay_intra = jnp.where(diff >= 0, jnp.exp(lg[:, None, None] * jnp.maximum(diff, 0.0)), 0.0)
    q_decay = jnp.exp(lg[:, None] * (idx + 1.0))
    k_decay = jnp.exp(lg[:, None] * (CHUNK - 1.0 - idx))
    chunk_decay = jnp.exp(lg * CHUNK)
    scores = jnp.einsum('bhncd,bhnmd->bhncm', qc, kc) * decay_intra[None, :, None]
    intra = jnp.einsum('bhncm,bhnmv->bhncv', scores, vc)
    kv_chunk = jnp.einsum('bhncd,bhncv->bhndv', kc * k_decay[None, :, None, :, None], vc)

    def step(s, kv_n):
        return s * chunk_decay[None, :, None, None] + kv_n, s

    _, s_before = lax.scan(step, s0.astype(jnp.float32), jnp.moveaxis(kv_chunk, 2, 0))
    cross = jnp.einsum('bhncd,nbhdv->bhncv', qc * q_decay[None, :, None, :, None], s_before)
    return (intra + cross).reshape(b, h, l, dv)


def retention_bidir(q, k, v, lg_fwd, lg_bwd, s_fwd, s_bwd):
    out_f = retention_chunked(q, k, v, lg_fwd, s_fwd)
    out_b = retention_chunked(jnp.flip(q, 2), jnp.flip(k, 2), jnp.flip(v, 2), lg_bwd, s_bwd)
    return out_f + jnp.flip(out_b, 2)


def context_states(kc, vc, lg_fwd, lg_bwd):
    lc = kc.shape[2]
    pos = jnp.arange(lc, dtype=jnp.float32)
    w_f = jnp.exp(lg_fwd.astype(jnp.float32)[:, None] * (lc - 1.0 - pos))
    w_b = jnp.exp(lg_bwd.astype(jnp.float32)[:, None] * pos)
    s_f = jnp.einsum('hl,bhlk,bhlv->bhkv', w_f, kc, vc)
    s_b = jnp.einsum('hl,bhlk,bhlv->bhkv', w_b, kc, vc)
    return s_f, s_b


def group_norm_heads(o):
    mu = jnp.mean(o, axis=-1, keepdims=True)
    var = jnp.mean(jnp.square(o - mu), axis=-1, keepdims=True)
    return (o - mu) * lax.rsqrt(var + GN_EPS)


def conv3_centred(u, w):
    up = jnp.pad(u, ((0, 0), (1, 1), (0, 0)))
    return up[:, :-2] * w[0] + up[:, 1:-1] * w[1] + up[:, 2:] * w[2]


def even_mixer(p, w_conv, w_out, lg_fwd, lg_bwd, s_fwd, s_bwd, rope):
    b, l, _ = p.shape
    q, k, v, g, bg, cg, u = jnp.split(
        p, [RET_W, 2 * RET_W, 3 * RET_W, 4 * RET_W, 4 * RET_W + CONV_W, 4 * RET_W + 2 * CONV_W], axis=-1)
    qh, kh, vh = to_heads(q), to_heads(k) * K_SCALE, to_heads(v)
    if rope is not None:
        qh, kh = apply_rope(qh, *rope), apply_rope(kh, *rope)
    o = group_norm_heads(retention_bidir(qh, kh, vh, lg_fwd, lg_bwd, s_fwd, s_bwd))
    ret = o.transpose(0, 2, 1, 3).reshape(b, l, RET_W).astype(p.dtype) * jax.nn.silu(g)
    conv = bg * conv3_centred(cg * u, w_conv)
    return jnp.concatenate([ret, conv], axis=-1) @ w_out


def centred_window_mean(u, w):
    l = u.shape[1]
    cs = jnp.cumsum(u.astype(jnp.float32), axis=1)
    cs = jnp.concatenate([jnp.zeros_like(cs[:, :1]), cs], axis=1)
    t = jnp.arange(l)
    lo = jnp.clip(t - w // 2, 0, l)
    hi = jnp.clip(t + (w - w // 2), 0, l)
    s = jnp.take(cs, hi, axis=1) - jnp.take(cs, lo, axis=1)
    cnt = (hi - lo).astype(jnp.float32)[None, :, None]
    return (s / cnt).astype(u.dtype)


def pool_mixer(h, w_groups, scale):
    b, l, d = h.shape
    hg = h.reshape(b, l, POOL_GROUPS, POOL_GROUP_W)
    pooled = jnp.stack([centred_window_mean(hg[:, :, i], w) - hg[:, :, i]
                        for i, w in enumerate(POOL_WINDOWS)], axis=2)
    y = jnp.einsum('blgc,gcd->blgd', pooled, w_groups).reshape(b, l, d)
    return y * scale


def setup_inputs(seed: int = 0) -> dict:
    key = jax.random.key(seed)
    ks = jax.random.split(key, 24)
    f32 = jnp.float32

    def nrm(k, shape, fan_in, scale=1.0):
        return jax.random.normal(k, shape, f32) * (scale * fan_in ** -0.5)

    gamma0 = 1.0 - 2.0 ** (-5.0 - np.arange(RET_HEADS, dtype=np.float32))
    decay_logit0 = jnp.asarray(np.log(gamma0 / (1.0 - gamma0)).astype(np.float32))
    return {
        "x": jax.random.normal(ks[0], (BATCH, SEQ, D_MODEL), f32),
        "c": jax.random.normal(ks[1], (BATCH, D_MODEL), f32),
        "ctx": jax.random.normal(ks[2], (BATCH, CTX_LEN, D_MODEL), f32),
        "c_ctx": jax.random.normal(ks[3], (D_MODEL,), f32),
        "w_mod": nrm(ks[4], (DEPTH, D_MODEL, N_MOD * D_MODEL), D_MODEL, 0.5),
        "b_mod": 0.02 * jax.random.normal(ks[5], (DEPTH, N_MOD * D_MODEL), f32),
        "norm_ffn1": 1.0 + 0.05 * jax.random.normal(ks[6], (DEPTH, D_MODEL), f32),
        "norm_mix": 1.0 + 0.05 * jax.random.normal(ks[7], (DEPTH, D_MODEL), f32),
        "norm_ffn2": 1.0 + 0.05 * jax.random.normal(ks[8], (DEPTH, D_MODEL), f32),
        "ffn1_w_gate": nrm(ks[9], (DEPTH, D_MODEL, D_FF), D_MODEL),
        "ffn1_w_up": nrm(ks[10], (DEPTH, D_MODEL, D_FF), D_MODEL),
        "ffn1_w_down": nrm(ks[11], (DEPTH, D_FF, D_MODEL), D_FF),
        "ffn2_w_gate": nrm(ks[12], (DEPTH, D_MODEL, D_FF), D_MODEL),
        "ffn2_w_up": nrm(ks[13], (DEPTH, D_MODEL, D_FF), D_MODEL),
        "ffn2_w_down": nrm(ks[14], (DEPTH, D_FF, D_MODEL), D_FF),
        "mix_w_in": nrm(ks[15], (N_EVEN, D_MODEL, IN_W), D_MODEL),
        "mix_w_conv": nrm(ks[16], (N_EVEN, CONV_WIDTH, CONV_W), CONV_WIDTH),
        "mix_w_out": nrm(ks[17], (N_EVEN, D_MODEL, D_MODEL), D_MODEL),
        "ret_decay_fwd": decay_logit0 + 0.1 * jax.random.normal(ks[18], (N_EVEN, RET_HEADS), f32),
        "ret_decay_bwd": decay_logit0 + 0.1 * jax.random.normal(ks[19], (N_EVEN, RET_HEADS), f32),
        "pool_w": nrm(ks[20], (N_ODD, POOL_GROUPS, POOL_GROUP_W, POOL_GROUP_W), POOL_GROUP_W),
        "pool_scale": 1.0 + 0.1 * jax.random.normal(ks[21], (N_ODD, D_MODEL), f32),
        "final_norm": 1.0 + 0.05 * jax.random.normal(ks[22], (D_MODEL,), f32),
    }


def reference(x, c, ctx, c_ctx, w_mod, b_mod, norm_ffn1, norm_mix, norm_ffn2,
              ffn1_w_gate, ffn1_w_up, ffn1_w_down, ffn2_w_gate, ffn2_w_up, ffn2_w_down,
              mix_w_in, mix_w_conv, mix_w_out, ret_decay_fwd, ret_decay_bwd,
              pool_w, pool_scale, final_norm):
    b, l, _ = x.shape
    ROWS = l // GRID_W
    rows = jnp.repeat(jnp.arange(ROWS), GRID_W)
    cols = jnp.tile(jnp.arange(GRID_W), ROWS)
    rope = axial_rope_tables(rows, cols)
    last_even = ((DEPTH - 1) // 2) * 2
    xc = ctx
    for li in range(DEPTH):
        ctx_needed = li <= last_even
        ctx_full = li < last_even
        m = adaln(c, w_mod[li], b_mod[li])
        x = ffn_sublayer(x, norm_ffn1[li], m[0], m[1], m[2], ffn1_w_gate[li], ffn1_w_up[li], ffn1_w_down[li])
        if ctx_needed:
            mc = adaln(c_ctx, w_mod[li], b_mod[li])
            xc = ffn_sublayer(xc, norm_ffn1[li], mc[0], mc[1], mc[2],
                              ffn1_w_gate[li], ffn1_w_up[li], ffn1_w_down[li])
        h = modulate(rmsnorm(x, norm_mix[li]), m[3], m[4])
        if li % 2 == 0:
            e = li // 2
            lg_f = jax.nn.log_sigmoid(ret_decay_fwd[e])
            lg_b = jax.nn.log_sigmoid(ret_decay_bwd[e])
            hc = modulate(rmsnorm(xc, norm_mix[li]), mc[3], mc[4])
            if ctx_full:
                pc = hc @ mix_w_in[e]
                kc, vc = pc[..., RET_W:2 * RET_W], pc[..., 2 * RET_W:3 * RET_W]
                zeros = jnp.zeros((b, RET_HEADS, RET_HEAD_DIM, RET_HEAD_DIM), jnp.float32)
                yc = even_mixer(pc, mix_w_conv[e], mix_w_out[e], lg_f, lg_b, zeros, zeros, None)
            else:
                kc = hc @ mix_w_in[e][:, RET_W:2 * RET_W]
                vc = hc @ mix_w_in[e][:, 2 * RET_W:3 * RET_W]
            s_f, s_b = context_states(to_heads(kc) * K_SCALE, to_heads(vc), lg_f, lg_b)
            y = even_mixer(h @ mix_w_in[e], mix_w_conv[e], mix_w_out[e], lg_f, lg_b, s_f, s_b, rope)
        else:
            o = li // 2
            y = pool_mixer(h, pool_w[o], pool_scale[o])
            if ctx_full:
                hc = modulate(rmsnorm(xc, norm_mix[li]), mc[3], mc[4])
                yc = pool_mixer(hc, pool_w[o], pool_scale[o])
        x = x + m[5] * y.astype(x.dtype)
        x = ffn_sublayer(x, norm_ffn2[li], m[6], m[7], m[8], ffn2_w_gate[li], ffn2_w_up[li], ffn2_w_down[li])
        if ctx_full:
            xc = xc + mc[5] * yc.astype(xc.dtype)
            xc = ffn_sublayer(xc, norm_ffn2[li], mc[6], mc[7], mc[8],
                              ffn2_w_gate[li], ffn2_w_up[li], ffn2_w_down[li])
    return rmsnorm(x, final_norm)
```

```python
import functools

import jax
import jax.numpy as jnp
from jax import lax
from jax.experimental import pallas as pl
from jax.experimental.pallas import tpu as pltpu

D_MODEL = 2048
GRID_W = 64
RET_HEADS = 8
RET_HEAD_DIM = 128
RET_W = RET_HEADS * RET_HEAD_DIM
K_SCALE = RET_HEAD_DIM ** -0.5
ROPE_BASE = 10000.0
CONV_W = D_MODEL - RET_W
POOL_WINDOWS = (2, 4, 8, 16)
POOL_GROUP_W = D_MODEL // len(POOL_WINDOWS)
N_MOD = 9
MACARON = 0.5
EPS = 1e-6
GN_EPS = 1e-5

F32 = jnp.float32
BF16 = jnp.bfloat16

V7X_VMEM_BYTES = 64 * 1024 * 1024
VMEM_LIMIT = V7X_VMEM_BYTES - 6 * 1024 * 1024

FFN_TOKEN_TILE = 1024
FFN_FF_TILE = 256
FFN_ROW_CHUNK = 128
FFN_DOWN_CHUNK = 512
ADALN_COL_TILE = 1024
NORM_ROW_TILE = 512
RET_CHUNK = 256
OUT_COL_TILE = 512


def _params(n_axes, vmem=VMEM_LIMIT):
    return pltpu.CompilerParams(dimension_semantics=("arbitrary",) * n_axes, vmem_limit_bytes=vmem)


def _silu(v):
    return v / (1.0 + jnp.exp(-v))


def _bdot(a, b):
    return jnp.dot(a.astype(BF16), b.astype(BF16), preferred_element_type=F32)


def _adaln_kernel(cond_ref, w_ref, b_ref, o_ref):
    s = _silu(cond_ref[...])
    o_ref[...] = _bdot(s, w_ref[...]) + b_ref[...]


def _adaln(cond8, w_mod, b_mod):
    depth, d, n = w_mod.shape
    tn = ADALN_COL_TILE
    return pl.pallas_call(
        _adaln_kernel,
        grid=(depth, n // tn),
        in_specs=[
            pl.BlockSpec((8, d), lambda l, j: (0, 0)),
            pl.BlockSpec((None, d, tn), lambda l, j: (l, 0, j)),
            pl.BlockSpec((None, 1, tn), lambda l, j: (l, 0, j)),
        ],
        out_specs=pl.BlockSpec((None, 8, tn), lambda l, j: (l, 0, j)),
        out_shape=jax.ShapeDtypeStruct((depth, 8, n), F32),
        compiler_params=_params(2),
        name="adaln",
    )(cond8, w_mod, b_mod.reshape(depth, 1, n))


def _rms_mod(x, g, shift, scale):
    n = x * lax.rsqrt(jnp.mean(x * x, axis=-1, keepdims=True) + EPS)
    return (n * g) * (1.0 + scale) + shift


def _ffn_kernel(x_ref, mod_ref, gn_ref, wg_ref, wu_ref, wd_ref, fn_ref, o_ref, h_ref, *, row0, final):
    j = pl.program_id(2)
    nj = pl.num_programs(2)
    tm = x_ref.shape[0]
    n_chunks = tm // FFN_ROW_CHUNK

    @pl.when(j == 0)
    def _():
        shift = mod_ref[row0:row0 + 1, :]
        scale = mod_ref[row0 + 1:row0 + 2, :]
        g = gn_ref[...]

        def body(r, carry):
            rows = pl.ds(pl.multiple_of(r * FFN_ROW_CHUNK, FFN_ROW_CHUNK), FFN_ROW_CHUNK)
            h_ref[rows, :] = _rms_mod(x_ref[rows, :], g, shift, scale).astype(BF16)
            o_ref[rows, :] = jnp.zeros((FFN_ROW_CHUNK, o_ref.shape[1]), F32)
            return carry

        lax.fori_loop(0, n_chunks, body, 0)

    h = h_ref[...]
    gate = jnp.dot(h, wg_ref[...].astype(BF16), preferred_element_type=F32)
    up = jnp.dot(h, wu_ref[...].astype(BF16), preferred_element_type=F32)
    a = (_silu(gate) * up).astype(BF16)
    for n0 in range(0, o_ref.shape[1], FFN_DOWN_CHUNK):
        cols = slice(n0, n0 + FFN_DOWN_CHUNK)
        o_ref[:, cols] += jnp.dot(a, wd_ref[:, cols].astype(BF16), preferred_element_type=F32)

    @pl.when(j == nj - 1)
    def _():
        res_gate = MACARON * mod_ref[row0 + 2:row0 + 3, :]
        fn = fn_ref[...]

        def body(r, carry):
            rows = pl.ds(pl.multiple_of(r * FFN_ROW_CHUNK, FFN_ROW_CHUNK), FFN_ROW_CHUNK)
            xn = x_ref[rows, :] + res_gate * o_ref[rows, :]
            if final:
                xn = xn * lax.rsqrt(jnp.mean(xn * xn, axis=-1, keepdims=True) + EPS) * fn
            o_ref[rows, :] = xn
            return carry

        lax.fori_loop(0, n_chunks, body, 0)


def _ffn(x, mod, g_norm, w_gate, w_up, w_down, final_norm, *, li, row0, final):
    nb, t, d = x.shape
    d_ff = w_gate.shape[2]
    tm = min(FFN_TOKEN_TILE, t)
    tf = FFN_FF_TILE
    kern = functools.partial(_ffn_kernel, row0=row0, final=final)
    return pl.pallas_call(
        kern,
        grid=(nb, t // tm, d_ff // tf),
        in_specs=[
            pl.BlockSpec((None, tm, d), lambda b, i, j: (b, i, 0)),
            pl.BlockSpec((None, N_MOD, d), lambda b, i, j: (b, 0, 0)),
            pl.BlockSpec((1, d), lambda b, i, j: (0, 0)),
            pl.BlockSpec((None, d, tf), lambda b, i, j: (li, 0, j)),
            pl.BlockSpec((None, d, tf), lambda b, i, j: (li, 0, j)),
            pl.BlockSpec((None, tf, d), lambda b, i, j: (li, j, 0)),
            pl.BlockSpec((1, d), lambda b, i, j: (0, 0)),
        ],
        out_specs=pl.BlockSpec((None, tm, d), lambda b, i, j: (b, i, 0)),
        out_shape=jax.ShapeDtypeStruct((nb, t, d), F32),
        scratch_shapes=[pltpu.VMEM((tm, d), BF16)],
        compiler_params=_params(3),
        name="ffn",
    )(x, mod, g_norm.reshape(1, d), w_gate, w_up, w_down, final_norm.reshape(1, d))


def _normmod_kernel(x_ref, mod_ref, gn_ref, o_ref, *, row0):
    shift = mod_ref[row0:row0 + 1, :]
    scale = mod_ref[row0 + 1:row0 + 2, :]
    o_ref[...] = _rms_mod(x_ref[...], gn_ref[...], shift, scale).astype(BF16)


def _normmod(x, mod, g_norm, *, row0):
    nb, t, d = x.shape
    tm = NORM_ROW_TILE
    return pl.pallas_call(
        functools.partial(_normmod_kernel, row0=row0),
        grid=(nb, t // tm),
        in_specs=[
            pl.BlockSpec((None, tm, d), lambda b, i: (b, i, 0)),
            pl.BlockSpec((None, N_MOD, d), lambda b, i: (b, 0, 0)),
            pl.BlockSpec((1, d), lambda b, i: (0, 0)),
        ],
        out_specs=pl.BlockSpec((None, tm, d), lambda b, i: (b, i, 0)),
        out_shape=jax.ShapeDtypeStruct((nb, t, d), BF16),
        compiler_params=_params(2),
        name="normmod",
    )(x, mod, g_norm.reshape(1, d))


def _ctx_kv_kernel(x_ref, mod_ref, gn_ref, w_ref, o_ref, h_ref):
    @pl.when(pl.program_id(0) == 0)
    def _():
        h_ref[...] = _rms_mod(x_ref[...], gn_ref[...], mod_ref[3:4, :], mod_ref[4:5, :]).astype(BF16)

    o_ref[...] = jnp.dot(h_ref[...], w_ref[...].astype(BF16), preferred_element_type=F32)


def _ctx_kv(xc, mod_ctx, g_norm, w_in, e):
    t, d = xc.shape
    tn = 256
    col0 = RET_W // tn
    return pl.pallas_call(
        _ctx_kv_kernel,
        grid=(2 * RET_W // tn,),
        in_specs=[
            pl.BlockSpec((t, d), lambda j: (0, 0)),
            pl.BlockSpec((N_MOD, d), lambda j: (0, 0)),
            pl.BlockSpec((1, d), lambda j: (0, 0)),
            pl.BlockSpec((None, d, tn), lambda j: (e, 0, col0 + j)),
        ],
        out_specs=pl.BlockSpec((t, tn), lambda j: (0, j)),
        out_shape=jax.ShapeDtypeStruct((t, 2 * RET_W), F32),
        scratch_shapes=[pltpu.VMEM((t, d), BF16)],
        compiler_params=_params(1),
        name="ctx_kv",
    )(xc, mod_ctx, g_norm.reshape(1, d), w_in)


def _ret_kernel(lgf_ref, lgb_ref, h_ref, wq_ref, wk_ref, wv_ref, wg_ref, kc_ref, vc_ref,
                cos_ref, sin_ref, o_ref, w_s, p_s):
    hd = RET_HEAD_DIM
    l = h_ref.shape[0]
    lc = kc_ref.shape[0]
    c = RET_CHUNK
    n_chunks = l // c
    lgf = lgf_ref[0:1, 0:1]
    lgb = lgb_ref[0:1, 0:1]

    w_s[:, 0 * hd:1 * hd] = wq_ref[...].astype(BF16)
    w_s[:, 1 * hd:2 * hd] = wk_ref[...].astype(BF16)
    w_s[:, 2 * hd:3 * hd] = wv_ref[...].astype(BF16)
    w_s[:, 3 * hd:4 * hd] = wg_ref[...].astype(BF16)
    p_s[...] = jnp.dot(h_ref[...], w_s[...], preferred_element_type=F32)

    cos2 = cos_ref[...]
    sin2 = sin_ref[...]
    q = p_s[:, 0 * hd:1 * hd]
    p_s[:, 0 * hd:1 * hd] = q * cos2 + pltpu.roll(q, hd // 2, axis=1) * sin2
    k = p_s[:, 1 * hd:2 * hd] * K_SCALE
    p_s[:, 1 * hd:2 * hd] = k * cos2 + pltpu.roll(k, hd // 2, axis=1) * sin2

    pos_c = lax.broadcasted_iota(jnp.int32, (1, lc), 1).astype(F32)
    kct = (kc_ref[...] * K_SCALE).T
    vcx = vc_ref[...]
    s_fwd = _bdot(kct * jnp.exp(lgf * (lc - 1.0 - pos_c)), vcx)
    s_bwd = _bdot(kct * jnp.exp(lgb * pos_c), vcx)

    pos = lax.broadcasted_iota(jnp.int32, (1, c), 1).astype(F32)
    wf_row = jnp.exp(lgf * (c - 1.0 - pos))
    wb_row = jnp.exp(lgb * pos)
    kv_f, kv_b = [], []
    for n in range(n_chunks):
        rows = slice(n * c, (n + 1) * c)
        kt = p_s[rows, 1 * hd:2 * hd].T
        kw = jnp.concatenate([kt * wf_row, kt * wb_row], axis=0)
        kv = _bdot(kw, p_s[rows, 2 * hd:3 * hd])
        kv_f.append(kv[:hd])
        kv_b.append(kv[hd:])

    gfc = jnp.exp(lgf * float(c))
    gbc = jnp.exp(lgb * float(c))
    st_f = [s_fwd]
    for n in range(n_chunks - 1):
        st_f.append(st_f[-1] * gfc + kv_f[n])
    st_b = [s_bwd]
    for n in range(n_chunks - 1, 0, -1):
        st_b.append(st_b[-1] * gbc + kv_b[n])
    st_b = st_b[::-1]

    ri = lax.broadcasted_iota(jnp.int32, (c, c), 0)
    ci = lax.broadcasted_iota(jnp.int32, (c, c), 1)
    diff = (ri - ci).astype(F32)
    decay = jnp.where(diff > 0, jnp.exp(lgf * jnp.maximum(diff, 0.0)),
                      jnp.where(diff < 0, jnp.exp(lgb * jnp.maximum(-diff, 0.0)), 2.0))
    rpos = lax.broadcasted_iota(jnp.int32, (c, 1), 0).astype(F32)
    qf_col = jnp.exp(lgf * (rpos + 1.0))
    qb_col = jnp.exp(lgb * (c - rpos))

    for n in range(n_chunks):
        rows = slice(n * c, (n + 1) * c)
        qn = p_s[rows, 0 * hd:1 * hd]
        kn = p_s[rows, 1 * hd:2 * hd]
        vn = p_s[rows, 2 * hd:3 * hd]
        gn = p_s[rows, 3 * hd:4 * hd]
        scores = lax.dot_general(qn.astype(BF16), kn.astype(BF16), (((1,), (1,)), ((), ())),
                                 preferred_element_type=F32)
        o = _bdot(scores * decay, vn)
        qd = jnp.concatenate([qn * qf_col, qn * qb_col], axis=1)
        st = jnp.concatenate([st_f[n], st_b[n]], axis=0)
        o = o + _bdot(qd, st)
        mu = jnp.mean(o, axis=-1, keepdims=True)
        oc = o - mu
        var = jnp.mean(oc * oc, axis=-1, keepdims=True)
        o_ref[rows, :] = (oc * lax.rsqrt(var + GN_EPS) * _silu(gn)).astype(BF16)


def _retention(h, w_in, e, kv_ctx, lg_f, lg_b, cos2, sin2):
    nb, l, d = h.shape
    lc = kv_ctx.shape[1]
    hd = RET_HEAD_DIM
    nh = RET_HEADS
    lgf = jnp.broadcast_to(lg_f.astype(F32)[:, None, None], (nh, 8, 128))
    lgb = jnp.broadcast_to(lg_b.astype(F32)[:, None, None], (nh, 8, 128))

    def wspec(part):
        return pl.BlockSpec((None, d, hd), lambda b, n: (e, 0, part * nh + n))

    return pl.pallas_call(
        _ret_kernel,
        grid=(nb, nh),
        in_specs=[
            pl.BlockSpec((None, 8, 128), lambda b, n: (n, 0, 0)),
            pl.BlockSpec((None, 8, 128), lambda b, n: (n, 0, 0)),
            pl.BlockSpec((None, l, d), lambda b, n: (b, 0, 0)),
            wspec(0), wspec(1), wspec(2), wspec(3),
            pl.BlockSpec((None, lc, hd), lambda b, n: (b, 0, n)),
            pl.BlockSpec((None, lc, hd), lambda b, n: (b, 0, nh + n)),
            pl.BlockSpec((l, hd), lambda b, n: (0, 0)),
            pl.BlockSpec((l, hd), lambda b, n: (0, 0)),
        ],
        out_specs=pl.BlockSpec((None, l, hd), lambda b, n: (b, 0, n)),
        out_shape=jax.ShapeDtypeStruct((nb, l, RET_W), BF16),
        scratch_shapes=[pltpu.VMEM((d, 4 * hd), BF16), pltpu.VMEM((l, 4 * hd), F32)],
        compiler_params=_params(2),
        name="retention",
    )(lgf, lgb, h, w_in, w_in, w_in, w_in, kv_ctx, kv_ctx, cos2, sin2)


CONV_COL_TILE = 256


def _conv_kernel(h_ref, wb_ref, wc_ref, wu_ref, wconv_ref, o_ref, w_s, p_s):
    tc = CONV_COL_TILE
    l = h_ref.shape[0]
    w_s[:, 0 * tc:1 * tc] = wb_ref[...].astype(BF16)
    w_s[:, 1 * tc:2 * tc] = wc_ref[...].astype(BF16)
    w_s[:, 2 * tc:3 * tc] = wu_ref[...].astype(BF16)
    p_s[...] = jnp.dot(h_ref[...], w_s[...], preferred_element_type=F32)
    z = p_s[:, 1 * tc:2 * tc] * p_s[:, 2 * tc:3 * tc]
    t = lax.broadcasted_iota(jnp.int32, (l, 1), 0)
    z_prev = jnp.where(t == 0, 0.0, pltpu.roll(z, 1, axis=0))
    z_next = jnp.where(t == l - 1, 0.0, pltpu.roll(z, l - 1, axis=0))
    conv = z_prev * wconv_ref[0:1, :] + z * wconv_ref[1:2, :] + z_next * wconv_ref[2:3, :]
    o_ref[...] = (p_s[:, 0 * tc:1 * tc] * conv).astype(BF16)


def _gated_conv(h, w_in, w_conv, e):
    nb, l, d = h.shape
    tc = CONV_COL_TILE
    nt = CONV_W // tc
    col0 = 4 * RET_W // tc

    def wspec(part):
        return pl.BlockSpec((None, d, tc), lambda b, n: (e, 0, col0 + part * nt + n))

    return pl.pallas_call(
        _conv_kernel,
        grid=(nb, nt),
        in_specs=[
            pl.BlockSpec((None, l, d), lambda b, n: (b, 0, 0)),
            wspec(0), wspec(1), wspec(2),
            pl.BlockSpec((None, 3, tc), lambda b, n: (e, 0, n)),
        ],
        out_specs=pl.BlockSpec((None, l, tc), lambda b, n: (b, 0, n)),
        out_shape=jax.ShapeDtypeStruct((nb, l, CONV_W), BF16),
        scratch_shapes=[pltpu.VMEM((d, 3 * tc), BF16), pltpu.VMEM((l, 3 * tc), F32)],
        compiler_params=_params(2),
        name="gated_conv",
    )(h, w_in, w_in, w_in, w_conv)


def _outproj_kernel(ar_ref, ac_ref, wr_ref, wc_ref, x_ref, mod_ref, o_ref):
    y = jnp.dot(ar_ref[...], wr_ref[...].astype(BF16), preferred_element_type=F32)
    y = y + jnp.dot(ac_ref[...], wc_ref[...].astype(BF16), preferred_element_type=F32)
    o_ref[...] = x_ref[...] + mod_ref[5:6, :] * y


def _outproj(a_ret, a_conv, w_out, e, x, mod):
    nb, l, d = x.shape
    tn = OUT_COL_TILE
    return pl.pallas_call(
        _outproj_kernel,
        grid=(nb, d // tn),
        in_specs=[
            pl.BlockSpec((None, l, RET_W), lambda b, n: (b, 0, 0)),
            pl.BlockSpec((None, l, CONV_W), lambda b, n: (b, 0, 0)),
            pl.BlockSpec((None, RET_W, tn), lambda b, n: (e, 0, n)),
            pl.BlockSpec((None, CONV_W, tn), lambda b, n: (e, 1, n)),
            pl.BlockSpec((None, l, tn), lambda b, n: (b, 0, n)),
            pl.BlockSpec((None, N_MOD, tn), lambda b, n: (b, 0, n)),
        ],
        out_specs=pl.BlockSpec((None, l, tn), lambda b, n: (b, 0, n)),
        out_shape=jax.ShapeDtypeStruct((nb, l, d), F32),
        compiler_params=_params(2),
        name="outproj",
    )(a_ret, a_conv, w_out, w_out, x, mod)


POOL_ROW_CHUNK = 256
POOL_HALO = 16


def _pool_kernel(h_ref, w_ref, ps_ref, x_ref, mod_ref, o_ref, hp_s, p_s):
    l, gw = h_ref.shape
    r, halo = POOL_ROW_CHUNK, POOL_HALO
    g = pl.program_id(1)
    lo = sum(jnp.where(g == gi, w // 2, 0) for gi, w in enumerate(POOL_WINDOWS))
    hi = sum(jnp.where(g == gi, w - w // 2, 0) for gi, w in enumerate(POOL_WINDOWS))

    hp_s[0:halo, :] = jnp.zeros((halo, gw), BF16)
    hp_s[halo + l:, :] = jnp.zeros((halo, gw), BF16)
    hp_s[halo:halo + l, :] = h_ref[...]

    ri = lax.broadcasted_iota(jnp.int32, (r, r + 2 * halo), 0)
    ci = lax.broadcasted_iota(jnp.int32, (r, r + 2 * halo), 1)
    off = ci - halo - ri
    band = jnp.where(off >= -lo, jnp.where(off < hi, 1.0, 0.0), 0.0).astype(BF16)
    for n in range(l // r):
        t = n * r + lax.broadcasted_iota(jnp.int32, (r, 1), 0)
        cnt = (jnp.minimum(t + hi, l) - jnp.maximum(t - lo, 0)).astype(F32)
        win = jnp.dot(band, hp_s[n * r:(n + 1) * r + 2 * halo, :], preferred_element_type=F32)
        hc = hp_s[n * r + halo:(n + 1) * r + halo, :].astype(F32)
        p_s[n * r:(n + 1) * r, :] = (win / cnt - hc).astype(BF16)

    y = jnp.dot(p_s[...], w_ref[...].astype(BF16), preferred_element_type=F32) * ps_ref[...]
    o_ref[...] = x_ref[...] + mod_ref[5:6, :] * y


def _pool(h, pool_w, pool_scale, o, x, mod):
    nb, l, d = x.shape
    gw = POOL_GROUP_W
    ng = d // gw
    assert max(POOL_WINDOWS) // 2 <= POOL_HALO and l % POOL_ROW_CHUNK == 0
    return pl.pallas_call(
        _pool_kernel,
        grid=(nb, ng),
        in_specs=[
            pl.BlockSpec((None, l, gw), lambda b, g: (b, 0, g)),
            pl.BlockSpec((None, None, gw, gw), lambda b, g: (o, g, 0, 0)),
            pl.BlockSpec((None, 1, gw), lambda b, g: (o, 0, g)),
            pl.BlockSpec((None, l, gw), lambda b, g: (b, 0, g)),
            pl.BlockSpec((None, N_MOD, gw), lambda b, g: (b, 0, g)),
        ],
        out_specs=pl.BlockSpec((None, l, gw), lambda b, g: (b, 0, g)),
        out_shape=jax.ShapeDtypeStruct((nb, l, d), F32),
        scratch_shapes=[pltpu.VMEM((l + 2 * POOL_HALO, gw), BF16), pltpu.VMEM((l, gw), BF16)],
        compiler_params=_params(2),
        name="pool",
    )(h, pool_w, pool_scale.reshape(-1, 1, d), x, mod)


def _rope_tables(l):
    quarter = RET_HEAD_DIM // 4
    inv = ROPE_BASE ** (-jnp.arange(quarter, dtype=F32) / quarter)
    rows = (jnp.arange(l) // GRID_W).astype(F32)
    cols = (jnp.arange(l) % GRID_W).astype(F32)
    ang = jnp.concatenate([rows[:, None] * inv, cols[:, None] * inv], axis=-1)
    cos, sin = jnp.cos(ang), jnp.sin(ang)
    return jnp.concatenate([cos, cos], axis=-1), jnp.concatenate([-sin, sin], axis=-1)


def kernel(x, c, ctx, c_ctx, w_mod, b_mod, norm_ffn1, norm_mix, norm_ffn2, ffn1_w_gate, ffn1_w_up, ffn1_w_down,
           ffn2_w_gate, ffn2_w_up, ffn2_w_down, mix_w_in, mix_w_conv, mix_w_out, ret_decay_fwd, ret_decay_bwd,
           pool_w, pool_scale, final_norm):
    b, l, d = x.shape
    depth = w_mod.shape[0]
    lc = ctx.shape[1]
    assert b + 1 <= 8 and d == D_MODEL
    last_even = ((depth - 1) // 2) * 2

    cond8 = jnp.zeros((8, d), F32).at[:b].set(c).at[b].set(c_ctx)
    mod_all = _adaln(cond8, w_mod, b_mod).reshape(depth, 8, N_MOD, d)
    cos2, sin2 = _rope_tables(l)

    xc = ctx.reshape(1, b * lc, d)
    for li in range(depth):
        mod = mod_all[li, :b]
        mod_ctx = mod_all[li, b:b + 1]
        ctx_needed = li <= last_even
        ctx_full = li < last_even
        x = _ffn(x, mod, norm_ffn1[li], ffn1_w_gate, ffn1_w_up, ffn1_w_down, final_norm,
                 li=li, row0=0, final=False)
        if ctx_needed:
            xc = _ffn(xc, mod_ctx, norm_ffn1[li], ffn1_w_gate, ffn1_w_up, ffn1_w_down, final_norm,
                      li=li, row0=0, final=False)
        h = _normmod(x, mod, norm_mix[li], row0=3)
        if li % 2 == 0:
            e = li // 2
            assert not ctx_full, "a context stream that is read again by a later layer is not implemented"
            lg_f = jax.nn.log_sigmoid(ret_decay_fwd[e])
            lg_b = jax.nn.log_sigmoid(ret_decay_bwd[e])
            kv_ctx = _ctx_kv(xc[0], mod_ctx[0], norm_mix[li], mix_w_in, e).reshape(b, lc, 2 * RET_W)
            a_ret = _retention(h, mix_w_in, e, kv_ctx, lg_f, lg_b, cos2, sin2)
            a_conv = _gated_conv(h, mix_w_in, mix_w_conv, e)
            x = _outproj(a_ret, a_conv, mix_w_out, e, x, mod)
        else:
            x = _pool(h, pool_w, pool_scale, li // 2, x, mod)
        x = _ffn(x, mod, norm_ffn2[li], ffn2_w_gate, ffn2_w_up, ffn2_w_down, final_norm,
                 li=li, row0=6, final=(li == depth - 1))
    return x
```

```python
import functools

import jax
import jax.numpy as jnp
from jax import lax
from jax.experimental import pallas as pl
from jax.experimental.pallas import tpu as pltpu

D_MODEL = 2048
GRID_W = 64
RET_HEADS = 8
RET_HEAD_DIM = 128
RET_W = RET_HEADS * RET_HEAD_DIM
K_SCALE = RET_HEAD_DIM ** -0.5
ROPE_BASE = 10000.0
CONV_W = D_MODEL - RET_W
POOL_WINDOWS = (2, 4, 8, 16)
POOL_GROUP_W = D_MODEL // len(POOL_WINDOWS)
N_MOD = 9
MACARON = 0.5
EPS = 1e-6
GN_EPS = 1e-5

F32 = jnp.float32
BF16 = jnp.bfloat16

V7X_VMEM_BYTES = 64 * 1024 * 1024
VMEM_LIMIT = V7X_VMEM_BYTES - 2 * 1024 * 1024

FFN_TOKEN_TILE = 1024
FFN_FF_TILE = 256
FFN_ROW_CHUNK = 128
FFN_DOWN_CHUNK = 512
NORM_COL_BLOCK = 128
ADALN_COL_TILE = 1024
CTX_KV_COL_TILE = 256
RET_CHUNK = 256
OUT_COL_TILE = 512


def _params(n_axes, vmem=VMEM_LIMIT):
    return pltpu.CompilerParams(dimension_semantics=("arbitrary",) * n_axes, vmem_limit_bytes=vmem)


def _silu(v):
    return v / (1.0 + jnp.exp(-v))


def _bdot(a, b):
    return jnp.dot(a.astype(BF16), b.astype(BF16), preferred_element_type=F32)


def _adaln_kernel(cond_ref, w_ref, b_ref, o_ref):
    s = _silu(cond_ref[...])
    o_ref[...] = _bdot(s, w_ref[...]) + b_ref[...]


def _adaln(cond8, w_mod, b_mod):
    depth, d, n = w_mod.shape
    tn = ADALN_COL_TILE
    return pl.pallas_call(
        _adaln_kernel,
        grid=(depth, n // tn),
        in_specs=[
            pl.BlockSpec((8, d), lambda l, j: (0, 0)),
            pl.BlockSpec((None, d, tn), lambda l, j: (l, 0, j)),
            pl.BlockSpec((None, 1, tn), lambda l, j: (l, 0, j)),
        ],
        out_specs=pl.BlockSpec((None, 8, tn), lambda l, j: (l, 0, j)),
        out_shape=jax.ShapeDtypeStruct((depth, 8, n), F32),
        compiler_params=_params(2),
        name="adaln",
    )(cond8, w_mod, b_mod.reshape(depth, 1, n))


def _col_blocks(d):
    return [slice(c0, c0 + NORM_COL_BLOCK) for c0 in range(0, d, NORM_COL_BLOCK)]


def _rstd_from_partial(acc, d):
    return lax.rsqrt(jnp.sum(acc, axis=-1, keepdims=True) * (1.0 / d) + EPS)


def _ffn_kernel(x_ref, mod_ref, gn_ref, wg_ref, wu_ref, wd_ref, pn_ref, o_ref, *rest, row0, post):
    if post == 'mix':
        hn_ref, h_ref = rest
    else:
        (h_ref,) = rest
    j = pl.program_id(2)
    nj = pl.num_programs(2)
    tm, d = x_ref.shape
    rc = FFN_ROW_CHUNK
    blocks = _col_blocks(d)

    def row_chunk(r):
        return pl.ds(pl.multiple_of(r * rc, rc), rc)

    @pl.when(j == 0)
    def _():
        shift = mod_ref[row0:row0 + 1, :]
        gs = gn_ref[...] * (1.0 + mod_ref[row0 + 1:row0 + 2, :])

        def body(r, carry):
            rows = row_chunk(r)
            acc = jnp.zeros((rc, NORM_COL_BLOCK), F32)
            for cols in blocks:
                xb = x_ref[rows, cols]
                acc = acc + xb * xb
            rstd = _rstd_from_partial(acc, d)
            for cols in blocks:
                h_ref[rows, cols] = (x_ref[rows, cols] * rstd * gs[:, cols] + shift[:, cols]).astype(BF16)
                o_ref[rows, cols] = jnp.zeros((rc, NORM_COL_BLOCK), F32)
            return carry

        lax.fori_loop(0, tm // rc, body, 0)

    h = h_ref[...]
    gate = jnp.dot(h, wg_ref[...].astype(BF16), preferred_element_type=F32)
    up = jnp.dot(h, wu_ref[...].astype(BF16), preferred_element_type=F32)
    a = (_silu(gate) * up).astype(BF16)

    for n0 in range(0, d, FFN_DOWN_CHUNK):
        cols = slice(n0, n0 + FFN_DOWN_CHUNK)
        o_ref[:, cols] += jnp.dot(a, wd_ref[:, cols].astype(BF16), preferred_element_type=F32)

    @pl.when(j == nj - 1)
    def _():
        res_gate = MACARON * mod_ref[row0 + 2:row0 + 3, :]
        pn = pn_ref[...]
        if post == 'mix':
            shift2 = mod_ref[3:4, :]
            gs2 = pn * (1.0 + mod_ref[4:5, :])

        def body(r, carry):
            rows = row_chunk(r)
            acc = jnp.zeros((rc, NORM_COL_BLOCK), F32)
            for cols in blocks:
                xn = x_ref[rows, cols] + res_gate[:, cols] * o_ref[rows, cols]
                o_ref[rows, cols] = xn
                if post is not None:
                    acc = acc + xn * xn
            if post is not None:
                rstd = _rstd_from_partial(acc, d)
                for cols in blocks:
                    if post == 'final':
                        o_ref[rows, cols] = o_ref[rows, cols] * rstd * pn[:, cols]
                    else:
                        hn_ref[rows, cols] = (o_ref[rows, cols] * rstd * gs2[:, cols] + shift2[:, cols]).astype(BF16)
            return carry

        lax.fori_loop(0, tm // rc, body, 0)


def _ffn(x, mod, g_norm, w_gate, w_up, w_down, post_norm, *, li, row0, post):
    nb, t, d = x.shape
    d_ff = w_gate.shape[2]
    tm = min(FFN_TOKEN_TILE, t)
    tf = FFN_FF_TILE
    out_specs = [pl.BlockSpec((None, tm, d), lambda b, i, j: (b, i, 0))]
    out_shape = [jax.ShapeDtypeStruct((nb, t, d), F32)]
    if post == 'mix':
        out_specs.append(pl.BlockSpec((None, tm, d), lambda b, i, j: (b, i, 0)))
        out_shape.append(jax.ShapeDtypeStruct((nb, t, d), BF16))
    out = pl.pallas_call(
        functools.partial(_ffn_kernel, row0=row0, post=post),
        grid=(nb, t // tm, d_ff // tf),
        in_specs=[
            pl.BlockSpec((None, tm, d), lambda b, i, j: (b, i, 0)),
            pl.BlockSpec((None, N_MOD, d), lambda b, i, j: (b, 0, 0)),
            pl.BlockSpec((1, d), lambda b, i, j: (0, 0)),
            pl.BlockSpec((None, d, tf), lambda b, i, j: (li, 0, j)),
            pl.BlockSpec((None, d, tf), lambda b, i, j: (li, 0, j)),
            pl.BlockSpec((None, tf, d), lambda b, i, j: (li, j, 0)),
            pl.BlockSpec((1, d), lambda b, i, j: (0, 0)),
        ],
        out_specs=out_specs,
        out_shape=out_shape,
        scratch_shapes=[pltpu.VMEM((tm, d), BF16)],
        compiler_params=_params(3),
        name="ffn",
    )(x, mod, g_norm.reshape(1, d), w_gate, w_up, w_down, post_norm.reshape(1, d))
    return out if post == 'mix' else out[0]


def _ctx_kv_kernel(h_ref, w_ref, o_ref):
    o_ref[...] = jnp.dot(h_ref[...], w_ref[...].astype(BF16), preferred_element_type=F32)


def _ctx_kv(hc, w_in, e):
    t, d = hc.shape
    tn = CTX_KV_COL_TILE
    col0 = RET_W // tn
    return pl.pallas_call(
        _ctx_kv_kernel,
        grid=(2 * RET_W // tn,),
        in_specs=[
            pl.BlockSpec((t, d), lambda j: (0, 0)),
            pl.BlockSpec((None, d, tn), lambda j: (e, 0, col0 + j)),
        ],
        out_specs=pl.BlockSpec((t, tn), lambda j: (0, j)),
        out_shape=jax.ShapeDtypeStruct((t, 2 * RET_W), F32),
        compiler_params=_params(1),
        name="ctx_kv",
    )(hc, w_in)


def _ret_kernel(lgf_ref, lgb_ref, h_ref, wq_ref, wk_ref, wv_ref, wg_ref, kc_ref, vc_ref,
                cos_ref, sin_ref, o_ref, w_s, p_s):
    hd = RET_HEAD_DIM
    l = h_ref.shape[0]
    lc = kc_ref.shape[0]
    c = RET_CHUNK
    n_chunks = l // c
    lgf = lgf_ref[0:1, 0:1]
    lgb = lgb_ref[0:1, 0:1]

    w_s[:, 0 * hd:1 * hd] = wq_ref[...].astype(BF16)
    w_s[:, 1 * hd:2 * hd] = wk_ref[...].astype(BF16)
    w_s[:, 2 * hd:3 * hd] = wv_ref[...].astype(BF16)
    w_s[:, 3 * hd:4 * hd] = wg_ref[...].astype(BF16)
    p_s[...] = jnp.dot(h_ref[...], w_s[...], preferred_element_type=F32)

    cos2 = cos_ref[...]
    sin2 = sin_ref[...]
    q = p_s[:, 0 * hd:1 * hd]
    p_s[:, 0 * hd:1 * hd] = q * cos2 + pltpu.roll(q, hd // 2, axis=1) * sin2
    k = p_s[:, 1 * hd:2 * hd] * K_SCALE
    p_s[:, 1 * hd:2 * hd] = k * cos2 + pltpu.roll(k, hd // 2, axis=1) * sin2

    pos_c = lax.broadcasted_iota(jnp.int32, (1, lc), 1).astype(F32)
    kct = (kc_ref[...] * K_SCALE).T
    vcx = vc_ref[...]
    s_fwd = _bdot(kct * jnp.exp(lgf * (lc - 1.0 - pos_c)), vcx)
    s_bwd = _bdot(kct * jnp.exp(lgb * pos_c), vcx)

    pos = lax.broadcasted_iota(jnp.int32, (1, c), 1).astype(F32)
    wf_row = jnp.exp(lgf * (c - 1.0 - pos))
    wb_row = jnp.exp(lgb * pos)
    kv_f, kv_b = [], []
    for n in range(n_chunks):
        rows = slice(n * c, (n + 1) * c)
        kt = p_s[rows, 1 * hd:2 * hd].T
        kw = jnp.concatenate([kt * wf_row, kt * wb_row], axis=0)
        kv = _bdot(kw, p_s[rows, 2 * hd:3 * hd])
        kv_f.append(kv[:hd])
        kv_b.append(kv[hd:])

    gfc = jnp.exp(lgf * float(c))
    gbc = jnp.exp(lgb * float(c))
    st_f = [s_fwd]
    for n in range(n_chunks - 1):
        st_f.append(st_f[-1] * gfc + kv_f[n])
    st_b = [s_bwd]
    for n in range(n_chunks - 1, 0, -1):
        st_b.append(st_b[-1] * gbc + kv_b[n])
    st_b = st_b[::-1]

    ri = lax.broadcasted_iota(jnp.int32, (c, c), 0)
    ci = lax.broadcasted_iota(jnp.int32, (c, c), 1)
    diff = (ri - ci).astype(F32)
    decay = jnp.where(diff > 0, jnp.exp(lgf * jnp.maximum(diff, 0.0)),
                      jnp.where(diff < 0, jnp.exp(lgb * jnp.maximum(-diff, 0.0)), 2.0))
    rpos = lax.broadcasted_iota(jnp.int32, (c, 1), 0).astype(F32)
    qf_col = jnp.exp(lgf * (rpos + 1.0))
    qb_col = jnp.exp(lgb * (c - rpos))

    for n in range(n_chunks):
        rows = slice(n * c, (n + 1) * c)
        qn = p_s[rows, 0 * hd:1 * hd]
        kn = p_s[rows, 1 * hd:2 * hd]
        vn = p_s[rows, 2 * hd:3 * hd]
        gn = p_s[rows, 3 * hd:4 * hd]
        scores = lax.dot_general(qn.astype(BF16), kn.astype(BF16), (((1,), (1,)), ((), ())),
                                 preferred_element_type=F32)
        o = _bdot(scores * decay, vn)
        qd = jnp.concatenate([qn * qf_col, qn * qb_col], axis=1)
        st = jnp.concatenate([st_f[n], st_b[n]], axis=0)
        o = o + _bdot(qd, st)
        mu = jnp.mean(o, axis=-1, keepdims=True)
        oc = o - mu
        var = jnp.mean(oc * oc, axis=-1, keepdims=True)
        o_ref[rows, :] = (oc * lax.rsqrt(var + GN_EPS) * _silu(gn)).astype(BF16)


def _retention(h, w_in, e, kv_ctx, lg_f, lg_b, cos2, sin2):
    nb, l, d = h.shape
    lc = kv_ctx.shape[1]
    hd = RET_HEAD_DIM
    nh = RET_HEADS
    lgf = jnp.broadcast_to(lg_f.astype(F32)[:, None, None], (nh, 8, 128))
    lgb = jnp.broadcast_to(lg_b.astype(F32)[:, None, None], (nh, 8, 128))

    def wspec(part):
        return pl.BlockSpec((None, d, hd), lambda b, n: (e, 0, part * nh + n))

    return pl.pallas_call(
        _ret_kernel,
        grid=(nb, nh),
        in_specs=[
            pl.BlockSpec((None, 8, 128), lambda b, n: (n, 0, 0)),
            pl.BlockSpec((None, 8, 128), lambda b, n: (n, 0, 0)),
            pl.BlockSpec((None, l, d), lambda b, n: (b, 0, 0)),
            wspec(0), wspec(1), wspec(2), wspec(3),
            pl.BlockSpec((None, lc, hd), lambda b, n: (b, 0, n)),
            pl.BlockSpec((None, lc, hd), lambda b, n: (b, 0, nh + n)),
            pl.BlockSpec((l, hd), lambda b, n: (0, 0)),
            pl.BlockSpec((l, hd), lambda b, n: (0, 0)),
        ],
        out_specs=pl.BlockSpec((None, l, hd), lambda b, n: (b, 0, n)),
        out_shape=jax.ShapeDtypeStruct((nb, l, RET_W), BF16),
        scratch_shapes=[pltpu.VMEM((d, 4 * hd), BF16), pltpu.VMEM((l, 4 * hd), F32)],
        compiler_params=_params(2),
        name="retention",
    )(lgf, lgb, h, w_in, w_in, w_in, w_in, kv_ctx, kv_ctx, cos2, sin2)


CONV_COL_TILE = 256


def _conv_kernel(h_ref, wb_ref, wc_ref, wu_ref, wconv_ref, o_ref, w_s, p_s):
    tc = CONV_COL_TILE
    l = h_ref.shape[0]
    w_s[:, 0 * tc:1 * tc] = wb_ref[...].astype(BF16)
    w_s[:, 1 * tc:2 * tc] = wc_ref[...].astype(BF16)
    w_s[:, 2 * tc:3 * tc] = wu_ref[...].astype(BF16)
    p_s[...] = jnp.dot(h_ref[...], w_s[...], preferred_element_type=F32)
    z = p_s[:, 1 * tc:2 * tc] * p_s[:, 2 * tc:3 * tc]
    t = lax.broadcasted_iota(jnp.int32, (l, 1), 0)
    z_prev = jnp.where(t == 0, 0.0, pltpu.roll(z, 1, axis=0))
    z_next = jnp.where(t == l - 1, 0.0, pltpu.roll(z, l - 1, axis=0))
    conv = z_prev * wconv_ref[0:1, :] + z * wconv_ref[1:2, :] + z_next * wconv_ref[2:3, :]
    o_ref[...] = (p_s[:, 0 * tc:1 * tc] * conv).astype(BF16)


def _gated_conv(h, w_in, w_conv, e):
    nb, l, d = h.shape
    tc = CONV_COL_TILE
    nt = CONV_W // tc
    col0 = 4 * RET_W // tc

    def wspec(part):
        return pl.BlockSpec((None, d, tc), lambda b, n: (e, 0, col0 + part * nt + n))

    return pl.pallas_call(
        _conv_kernel,
        grid=(nb, nt),
        in_specs=[
            pl.BlockSpec((None, l, d), lambda b, n: (b, 0, 0)),
            wspec(0), wspec(1), wspec(2),
            pl.BlockSpec((None, 3, tc), lambda b, n: (e, 0, n)),
        ],
        out_specs=pl.BlockSpec((None, l, tc), lambda b, n: (b, 0, n)),
        out_shape=jax.ShapeDtypeStruct((nb, l, CONV_W), BF16),
        scratch_shapes=[pltpu.VMEM((d, 3 * tc), BF16), pltpu.VMEM((l, 3 * tc), F32)],
        compiler_params=_params(2),
        name="gated_conv",
    )(h, w_in, w_in, w_in, w_conv)


def _outproj_kernel(ar_ref, ac_ref, wr_ref, wc_ref, x_ref, mod_ref, o_ref):
    y = jnp.dot(ar_ref[...], wr_ref[...].astype(BF16), preferred_element_type=F32)
    y = y + jnp.dot(ac_ref[...], wc_ref[...].astype(BF16), preferred_element_type=F32)
    o_ref[...] = x_ref[...] + mod_ref[5:6, :] * y


def _outproj(a_ret, a_conv, w_out, e, x, mod):
    nb, l, d = x.shape
    tn = OUT_COL_TILE
    return pl.pallas_call(
        _outproj_kernel,
        grid=(nb, d // tn),
        in_specs=[
            pl.BlockSpec((None, l, RET_W), lambda b, n: (b, 0, 0)),
            pl.BlockSpec((None, l, CONV_W), lambda b, n: (b, 0, 0)),
            pl.BlockSpec((None, RET_W, tn), lambda b, n: (e, 0, n)),
            pl.BlockSpec((None, CONV_W, tn), lambda b, n: (e, 1, n)),
            pl.BlockSpec((None, l, tn), lambda b, n: (b, 0, n)),
            pl.BlockSpec((None, N_MOD, tn), lambda b, n: (b, 0, n)),
        ],
        out_specs=pl.BlockSpec((None, l, tn), lambda b, n: (b, 0, n)),
        out_shape=jax.ShapeDtypeStruct((nb, l, d), F32),
        compiler_params=_params(2),
        name="outproj",
    )(a_ret, a_conv, w_out, w_out, x, mod)


POOL_ROW_CHUNK = 256
POOL_HALO = 16


def _pool_kernel(h_ref, w_ref, ps_ref, x_ref, mod_ref, o_ref, hp_s, p_s):
    l, gw = h_ref.shape
    r, halo = POOL_ROW_CHUNK, POOL_HALO
    g = pl.program_id(1)
    lo = sum(jnp.where(g == gi, w // 2, 0) for gi, w in enumerate(POOL_WINDOWS))
    hi = sum(jnp.where(g == gi, w - w // 2, 0) for gi, w in enumerate(POOL_WINDOWS))

    hp_s[0:halo, :] = jnp.zeros((halo, gw), BF16)
    hp_s[halo + l:, :] = jnp.zeros((halo, gw), BF16)
    hp_s[halo:halo + l, :] = h_ref[...]

    ri = lax.broadcasted_iota(jnp.int32, (r, r + 2 * halo), 0)
    ci = lax.broadcasted_iota(jnp.int32, (r, r + 2 * halo), 1)
    off = ci - halo - ri
    band = jnp.where(off >= -lo, jnp.where(off < hi, 1.0, 0.0), 0.0).astype(BF16)
    for n in range(l // r):
        t = n * r + lax.broadcasted_iota(jnp.int32, (r, 1), 0)
        cnt = (jnp.minimum(t + hi, l) - jnp.maximum(t - lo, 0)).astype(F32)
        win = jnp.dot(band, hp_s[n * r:(n + 1) * r + 2 * halo, :], preferred_element_type=F32)
        hc = hp_s[n * r + halo:(n + 1) * r + halo, :].astype(F32)
        p_s[n * r:(n + 1) * r, :] = (win / cnt - hc).astype(BF16)

    y = jnp.dot(p_s[...], w_ref[...].astype(BF16), preferred_element_type=F32) * ps_ref[...]
    o_ref[...] = x_ref[...] + mod_ref[5:6, :] * y


def _pool(h, pool_w, pool_scale, o, x, mod):
    nb, l, d = x.shape
    gw = POOL_GROUP_W
    ng = d // gw
    assert max(POOL_WINDOWS) // 2 <= POOL_HALO and l % POOL_ROW_CHUNK == 0
    return pl.pallas_call(
        _pool_kernel,
        grid=(nb, ng),
        in_specs=[
            pl.BlockSpec((None, l, gw), lambda b, g: (b, 0, g)),
            pl.BlockSpec((None, None, gw, gw), lambda b, g: (o, g, 0, 0)),
            pl.BlockSpec((None, 1, gw), lambda b, g: (o, 0, g)),
            pl.BlockSpec((None, l, gw), lambda b, g: (b, 0, g)),
            pl.BlockSpec((None, N_MOD, gw), lambda b, g: (b, 0, g)),
        ],
        out_specs=pl.BlockSpec((None, l, gw), lambda b, g: (b, 0, g)),
        out_shape=jax.ShapeDtypeStruct((nb, l, d), F32),
        scratch_shapes=[pltpu.VMEM((l + 2 * POOL_HALO, gw), BF16), pltpu.VMEM((l, gw), BF16)],
        compiler_params=_params(2),
        name="pool",
    )(h, pool_w, pool_scale.reshape(-1, 1, d), x, mod)


def _rope_tables(l):
    quarter = RET_HEAD_DIM // 4
    inv = ROPE_BASE ** (-jnp.arange(quarter, dtype=F32) / quarter)
    rows = (jnp.arange(l) // GRID_W).astype(F32)
    cols = (jnp.arange(l) % GRID_W).astype(F32)
    ang = jnp.concatenate([rows[:, None] * inv, cols[:, None] * inv], axis=-1)
    cos, sin = jnp.cos(ang), jnp.sin(ang)
    return jnp.concatenate([cos, cos], axis=-1), jnp.concatenate([-sin, sin], axis=-1)


def kernel(x, c, ctx, c_ctx, w_mod, b_mod, norm_ffn1, norm_mix, norm_ffn2, ffn1_w_gate, ffn1_w_up, ffn1_w_down,
           ffn2_w_gate, ffn2_w_up, ffn2_w_down, mix_w_in, mix_w_conv, mix_w_out, ret_decay_fwd, ret_decay_bwd,
           pool_w, pool_scale, final_norm):
    b, l, d = x.shape
    depth = w_mod.shape[0]
    lc = ctx.shape[1]
    assert b + 1 <= 8 and d == D_MODEL
    last_even = ((depth - 1) // 2) * 2

    cond8 = jnp.zeros((8, d), F32).at[:b].set(c).at[b].set(c_ctx)
    mod_all = _adaln(cond8, w_mod, b_mod).reshape(depth, 8, N_MOD, d)
    cos2, sin2 = _rope_tables(l)

    xc = ctx.reshape(1, b * lc, d)
    for li in range(depth):
        mod = mod_all[li, :b]
        mod_ctx = mod_all[li, b:b + 1]
        ctx_needed = li <= last_even
        ctx_full = li < last_even
        x, h = _ffn(x, mod, norm_ffn1[li], ffn1_w_gate, ffn1_w_up, ffn1_w_down, norm_mix[li],
                    li=li, row0=0, post='mix')
        if ctx_needed:
            xc, hc = _ffn(xc, mod_ctx, norm_ffn1[li], ffn1_w_gate, ffn1_w_up, ffn1_w_down, norm_mix[li],
                          li=li, row0=0, post='mix')
        if li % 2 == 0:
            e = li // 2
            assert not ctx_full, "a context stream that is read again by a later layer is not implemented"
            lg_f = jax.nn.log_sigmoid(ret_decay_fwd[e])
            lg_b = jax.nn.log_sigmoid(ret_decay_bwd[e])
            kv_ctx = _ctx_kv(hc[0], mix_w_in, e).reshape(b, lc, 2 * RET_W)
            a_ret = _retention(h, mix_w_in, e, kv_ctx, lg_f, lg_b, cos2, sin2)
            a_conv = _gated_conv(h, mix_w_in, mix_w_conv, e)
            x = _outproj(a_ret, a_conv, mix_w_out, e, x, mod)
        else:
            x = _pool(h, pool_w, pool_scale, li // 2, x, mod)
        x = _ffn(x, mod, norm_ffn2[li], ffn2_w_gate, ffn2_w_up, ffn2_w_down, final_norm,
                 li=li, row0=6, post=('final' if li == depth - 1 else None))
    return x
```

```python
import functools

import jax
import jax.numpy as jnp
from jax import lax
from jax.experimental import pallas as pl
from jax.experimental.pallas import tpu as pltpu

D_MODEL = 2048
GRID_W = 64
RET_HEADS = 8
RET_HEAD_DIM = 128
RET_W = RET_HEADS * RET_HEAD_DIM
K_SCALE = RET_HEAD_DIM ** -0.5
ROPE_BASE = 10000.0
CONV_W = D_MODEL - RET_W
POOL_WINDOWS = (2, 4, 8, 16)
POOL_GROUP_W = D_MODEL // len(POOL_WINDOWS)
N_MOD = 9
MACARON = 0.5
EPS = 1e-6
GN_EPS = 1e-5

F32 = jnp.float32
BF16 = jnp.bfloat16

V7X_VMEM_BYTES = 64 * 1024 * 1024
VMEM_LIMIT = V7X_VMEM_BYTES - 7 * 1024 * 1024

FFN_TOKEN_TILE = 1024
FFN_FF_TILE = 256
FFN_FF_TILE_BF16 = 512
FFN_ROW_CHUNK = 128
FFN_DOWN_CHUNK = 512
NORM_COL_BLOCK = 128
ADALN_COL_TILE = 1024
CTX_KV_COL_TILE = 256
RET_CHUNK = 256
RET_HEADS_PER_STEP = 2
OUT_COL_TILE = 512


def _params(n_axes, vmem=VMEM_LIMIT):
    return pltpu.CompilerParams(dimension_semantics=("arbitrary",) * n_axes, vmem_limit_bytes=vmem)


def _silu(v):
    return v / (1.0 + jnp.exp(-v))


def _bdot(a, b):
    return jnp.dot(a.astype(BF16), b.astype(BF16), preferred_element_type=F32)


def _adaln_kernel(cond_ref, w_ref, b_ref, o_ref):
    s = _silu(cond_ref[...])
    o_ref[...] = _bdot(s, w_ref[...]) + b_ref[...]


def _adaln(cond8, w_mod, b_mod):
    depth, d, n = w_mod.shape
    tn = ADALN_COL_TILE
    return pl.pallas_call(
        _adaln_kernel,
        grid=(depth, n // tn),
        in_specs=[
            pl.BlockSpec((8, d), lambda l, j: (0, 0)),
            pl.BlockSpec((None, d, tn), lambda l, j: (l, 0, j)),
            pl.BlockSpec((None, 1, tn), lambda l, j: (l, 0, j)),
        ],
        out_specs=pl.BlockSpec((None, 8, tn), lambda l, j: (l, 0, j)),
        out_shape=jax.ShapeDtypeStruct((depth, 8, n), F32),
        compiler_params=_params(2),
        name="adaln",
    )(cond8, w_mod, b_mod.reshape(depth, 1, n))


def _col_blocks(d):
    return [slice(c0, c0 + NORM_COL_BLOCK) for c0 in range(0, d, NORM_COL_BLOCK)]


def _rstd_from_partial(acc, d):
    return lax.rsqrt(jnp.sum(acc, axis=-1, keepdims=True) * (1.0 / d) + EPS)


def _ffn_kernel(*refs, row0, post, emit_w, n_in):
    x_ref, mod_ref, gn_ref, wg_ref, wu_ref, wd_ref, pn_ref = refs[:7]
    refs = list(refs[n_in:])
    o_ref = refs.pop(0)
    hn_ref = refs.pop(0) if post == 'mix' else None
    wg16_ref, wu16_ref, wd16_ref = (refs.pop(0), refs.pop(0), refs.pop(0)) if emit_w else (None, None, None)
    (h_ref,) = refs if post != 'mix' else (hn_ref,)
    j = pl.program_id(1)
    nj = pl.num_programs(1)
    tm, d = x_ref.shape
    rc = FFN_ROW_CHUNK
    blocks = _col_blocks(d)

    def row_chunk(r):
        return pl.ds(pl.multiple_of(r * rc, rc), rc)

    @pl.when(j == 0)
    def _():
        shift = mod_ref[row0:row0 + 1, :]
        gs = gn_ref[...] * (1.0 + mod_ref[row0 + 1:row0 + 2, :])

        def body(r, carry):
            rows = row_chunk(r)
            acc = jnp.zeros((rc, NORM_COL_BLOCK), F32)
            for cols in blocks:
                xb = x_ref[rows, cols]
                acc = acc + xb * xb
            rstd = _rstd_from_partial(acc, d)
            for cols in blocks:
                h_ref[rows, cols] = (x_ref[rows, cols] * rstd * gs[:, cols] + shift[:, cols]).astype(BF16)
                o_ref[rows, cols] = jnp.zeros((rc, NORM_COL_BLOCK), F32)
            return carry

        lax.fori_loop(0, tm // rc, body, 0)

    acts = []
    for f0 in range(0, wg_ref.shape[1], FFN_FF_TILE):
        hid = slice(f0, f0 + FFN_FF_TILE)
        wg = wg_ref[:, hid].astype(BF16)
        wu = wu_ref[:, hid].astype(BF16)
        if emit_w:
            wg16_ref[:, hid] = wg
            wu16_ref[:, hid] = wu
        gate = jnp.dot(h_ref[...], wg, preferred_element_type=F32)
        up = jnp.dot(h_ref[...], wu, preferred_element_type=F32)
        acts.append((_silu(gate) * up).astype(BF16))
    a = acts[0] if len(acts) == 1 else jnp.concatenate(acts, axis=1)
    for n0 in range(0, d, FFN_DOWN_CHUNK):
        cols = slice(n0, n0 + FFN_DOWN_CHUNK)
        wd = wd_ref[:, cols].astype(BF16)
        if emit_w:
            wd16_ref[:, cols] = wd
        o_ref[:, cols] += jnp.dot(a, wd, preferred_element_type=F32)

    @pl.when(j == nj - 1)
    def _():
        res_gate = MACARON * mod_ref[row0 + 2:row0 + 3, :]
        pn = pn_ref[...]
        if post == 'mix':
            shift2 = mod_ref[3:4, :]
            gs2 = pn * (1.0 + mod_ref[4:5, :])

        def body(r, carry):
            rows = row_chunk(r)
            acc = jnp.zeros((rc, NORM_COL_BLOCK), F32)
            for cols in blocks:
                xn = x_ref[rows, cols] + res_gate[:, cols] * o_ref[rows, cols]
                o_ref[rows, cols] = xn
                if post is not None:
                    acc = acc + xn * xn
            if post is not None:
                rstd = _rstd_from_partial(acc, d)
                for cols in blocks:
                    if post == 'final':
                        o_ref[rows, cols] = o_ref[rows, cols] * rstd * pn[:, cols]
                    else:
                        hn_ref[rows, cols] = (o_ref[rows, cols] * rstd * gs2[:, cols] + shift2[:, cols]).astype(BF16)
            return carry

        lax.fori_loop(0, tm // rc, body, 0)


def _ffn(x, mod, g_norm, weights, post_norm, *, row0, post, tiles=None, li=None, hn_buf=None):
    nb, t, d = x.shape
    tm = min(FFN_TOKEN_TILE, t)
    tpb = t // tm
    emit_w = li is not None
    inplace = tiles is not None
    t0, nt = tiles if inplace else (0, nb * tpb)
    w_gate, w_up, w_down = weights
    d_ff = w_gate.shape[-1]
    tf = FFN_FF_TILE if emit_w else FFN_FF_TILE_BF16

    def rows(n, j):
        return ((t0 + n) // tpb, (t0 + n) % tpb, 0)

    if emit_w:
        w_specs = [pl.BlockSpec((None, d, tf), lambda n, j: (li, 0, j)),
                   pl.BlockSpec((None, d, tf), lambda n, j: (li, 0, j)),
                   pl.BlockSpec((None, tf, d), lambda n, j: (li, j, 0))]
    else:
        w_specs = [pl.BlockSpec((d, tf), lambda n, j: (0, j)),
                   pl.BlockSpec((d, tf), lambda n, j: (0, j)),
                   pl.BlockSpec((tf, d), lambda n, j: (j, 0))]
    in_specs = [pl.BlockSpec((None, tm, d), rows),
                pl.BlockSpec((None, N_MOD, d), lambda n, j: ((t0 + n) // tpb, 0, 0)),
                pl.BlockSpec((1, d), lambda n, j: (0, 0)),
                *w_specs,
                pl.BlockSpec((1, d), lambda n, j: (0, 0))]
    args = [x, mod, g_norm.reshape(1, d), w_gate, w_up, w_down, post_norm.reshape(1, d)]
    out_specs = [pl.BlockSpec((None, tm, d), rows)]
    out_shape = [jax.ShapeDtypeStruct((nb, t, d), F32)]
    aliases = {0: 0} if inplace else {}
    if post == 'mix':
        out_specs.append(pl.BlockSpec((None, tm, d), rows))
        out_shape.append(jax.ShapeDtypeStruct((nb, t, d), BF16))
        if inplace:
            in_specs.append(pl.BlockSpec(memory_space=pl.ANY))
            args.append(hn_buf)
            aliases[len(args) - 1] = 1
    if emit_w:
        out_specs += [pl.BlockSpec((d, tf), lambda n, j: (0, j)),
                      pl.BlockSpec((d, tf), lambda n, j: (0, j)),
                      pl.BlockSpec((tf, d), lambda n, j: (j, 0))]
        out_shape += [jax.ShapeDtypeStruct((d, d_ff), BF16), jax.ShapeDtypeStruct((d, d_ff), BF16),
                      jax.ShapeDtypeStruct((d_ff, d), BF16)]
        assert nt == 1, "the bf16 weight casts are emitted once, by a single-tile call"
    out = pl.pallas_call(
        functools.partial(_ffn_kernel, row0=row0, post=post, emit_w=emit_w, n_in=len(args)),
        grid=(nt, d_ff // tf),
        in_specs=in_specs,
        out_specs=out_specs,
        out_shape=out_shape,
        input_output_aliases=aliases,
        scratch_shapes=[] if post == 'mix' else [pltpu.VMEM((tm, d), BF16)],
        compiler_params=_params(2),
        name="ffn",
    )(*args)
    hn = out[1] if post == 'mix' else None
    w16 = tuple(out[-3:]) if emit_w else None
    return out[0], hn, w16


def _ffn_stream(x, mod, g_norm, weights, post_norm, *, li, row0, post, hn_buf=None):
    nb, t, d = x.shape
    n_tiles = nb * (t // min(FFN_TOKEN_TILE, t))
    if post == 'mix' and hn_buf is None:
        hn_buf = jnp.zeros((nb, t, d), BF16)
    x, hn, w16 = _ffn(x, mod, g_norm, weights, post_norm, row0=row0, post=post, tiles=(0, 1), li=li, hn_buf=hn_buf)
    if n_tiles > 1:
        x, hn, _ = _ffn(x, mod, g_norm, w16, post_norm, row0=row0, post=post, tiles=(1, n_tiles - 1), hn_buf=hn)
    return x, hn


def _ctx_kv_kernel(h_ref, w_ref, o_ref):
    o_ref[...] = jnp.dot(h_ref[...], w_ref[...].astype(BF16), preferred_element_type=F32)


def _ctx_kv(hc, w_in, e):
    t, d = hc.shape
    tn = CTX_KV_COL_TILE
    col0 = RET_W // tn
    return pl.pallas_call(
        _ctx_kv_kernel,
        grid=(2 * RET_W // tn,),
        in_specs=[
            pl.BlockSpec((t, d), lambda j: (0, 0)),
            pl.BlockSpec((None, d, tn), lambda j: (e, 0, col0 + j)),
        ],
        out_specs=pl.BlockSpec((t, tn), lambda j: (0, j)),
        out_shape=jax.ShapeDtypeStruct((t, 2 * RET_W), F32),
        compiler_params=_params(1),
        name="ctx_kv",
    )(hc, w_in)


def _ret_kernel(lgf_ref, lgb_ref, h_ref, wq_ref, wk_ref, wv_ref, wg_ref, kc_ref, vc_ref,
                cos_ref, sin_ref, o_ref, w_s, p_s):
    hd = RET_HEAD_DIM
    for t in range(RET_HEADS_PER_STEP):
        cols = slice(t * hd, (t + 1) * hd)
        w_s[t, :, 0 * hd:1 * hd] = wq_ref[:, cols].astype(BF16)
        w_s[t, :, 1 * hd:2 * hd] = wk_ref[:, cols].astype(BF16)
        w_s[t, :, 2 * hd:3 * hd] = wv_ref[:, cols].astype(BF16)
        w_s[t, :, 3 * hd:4 * hd] = wg_ref[:, cols].astype(BF16)
        p_s[t] = jnp.dot(h_ref[...], w_s[t], preferred_element_type=F32)
    for t in range(RET_HEADS_PER_STEP):
        cols = slice(t * hd, (t + 1) * hd)
        _ret_head(lgf_ref[t, 0:1, 0:1], lgb_ref[t, 0:1, 0:1], p_s.at[t], kc_ref[:, cols], vc_ref[:, cols],
                  cos_ref[...], sin_ref[...], o_ref, cols)


def _ret_head(lgf, lgb, p_s, kc, vcx, cos2, sin2, o_ref, out_cols):
    hd = RET_HEAD_DIM
    l = p_s.shape[0]
    lc = kc.shape[0]
    c = RET_CHUNK
    n_chunks = l // c

    q = p_s[:, 0 * hd:1 * hd]
    p_s[:, 0 * hd:1 * hd] = q * cos2 + pltpu.roll(q, hd // 2, axis=1) * sin2
    k = p_s[:, 1 * hd:2 * hd] * K_SCALE
    p_s[:, 1 * hd:2 * hd] = k * cos2 + pltpu.roll(k, hd // 2, axis=1) * sin2

    pos_c = lax.broadcasted_iota(jnp.int32, (1, lc), 1).astype(F32)
    kct = (kc * K_SCALE).T
    s_fwd = _bdot(kct * jnp.exp(lgf * (lc - 1.0 - pos_c)), vcx)
    s_bwd = _bdot(kct * jnp.exp(lgb * pos_c), vcx)

    pos = lax.broadcasted_iota(jnp.int32, (1, c), 1).astype(F32)
    wf_row = jnp.exp(lgf * (c - 1.0 - pos))
    wb_row = jnp.exp(lgb * pos)
    kv_f, kv_b = [], []
    for n in range(n_chunks):
        rows = slice(n * c, (n + 1) * c)
        kt = p_s[rows, 1 * hd:2 * hd].T
        kw = jnp.concatenate([kt * wf_row, kt * wb_row], axis=0)
        kv = _bdot(kw, p_s[rows, 2 * hd:3 * hd])
        kv_f.append(kv[:hd])
        kv_b.append(kv[hd:])

    gfc = jnp.exp(lgf * float(c))
    gbc = jnp.exp(lgb * float(c))
    st_f = [s_fwd]
    for n in range(n_chunks - 1):
        st_f.append(st_f[-1] * gfc + kv_f[n])
    st_b = [s_bwd]
    for n in range(n_chunks - 1, 0, -1):
        st_b.append(st_b[-1] * gbc + kv_b[n])
    st_b = st_b[::-1]

    ri = lax.broadcasted_iota(jnp.int32, (c, c), 0)
    ci = lax.broadcasted_iota(jnp.int32, (c, c), 1)
    diff = (ri - ci).astype(F32)
    decay = jnp.where(diff > 0, jnp.exp(lgf * jnp.maximum(diff, 0.0)),
                      jnp.where(diff < 0, jnp.exp(lgb * jnp.maximum(-diff, 0.0)), 2.0))
    rpos = lax.broadcasted_iota(jnp.int32, (c, 1), 0).astype(F32)
    qf_col = jnp.exp(lgf * (rpos + 1.0))
    qb_col = jnp.exp(lgb * (c - rpos))

    for n in range(n_chunks):
        rows = slice(n * c, (n + 1) * c)
        qn = p_s[rows, 0 * hd:1 * hd]
        kn = p_s[rows, 1 * hd:2 * hd]
        vn = p_s[rows, 2 * hd:3 * hd]
        gn = p_s[rows, 3 * hd:4 * hd]
        scores = lax.dot_general(qn.astype(BF16), kn.astype(BF16), (((1,), (1,)), ((), ())),
                                 preferred_element_type=F32)
        o = _bdot(scores * decay, vn)
        qd = jnp.concatenate([qn * qf_col, qn * qb_col], axis=1)
        st = jnp.concatenate([st_f[n], st_b[n]], axis=0)
        o = o + _bdot(qd, st)
        mu = jnp.mean(o, axis=-1, keepdims=True)
        oc = o - mu
        var = jnp.mean(oc * oc, axis=-1, keepdims=True)
        o_ref[rows, out_cols] = (oc * lax.rsqrt(var + GN_EPS) * _silu(gn)).astype(BF16)


def _retention(h, w_in, e, kv_ctx, lg_f, lg_b, cos2, sin2):
    nb, l, d = h.shape
    lc = kv_ctx.shape[1]
    hd = RET_HEAD_DIM
    g = RET_HEADS_PER_STEP
    ng = RET_HEADS // g
    lgf = jnp.broadcast_to(lg_f.astype(F32)[:, None, None], (RET_HEADS, 8, 128))
    lgb = jnp.broadcast_to(lg_b.astype(F32)[:, None, None], (RET_HEADS, 8, 128))

    def wspec(part):
        return pl.BlockSpec((None, d, g * hd), lambda b, n: (e, 0, part * ng + n))

    return pl.pallas_call(
        _ret_kernel,
        grid=(nb, ng),
        in_specs=[
            pl.BlockSpec((g, 8, 128), lambda b, n: (n, 0, 0)),
            pl.BlockSpec((g, 8, 128), lambda b, n: (n, 0, 0)),
            pl.BlockSpec((None, l, d), lambda b, n: (b, 0, 0)),
            wspec(0), wspec(1), wspec(2), wspec(3),
            pl.BlockSpec((None, lc, g * hd), lambda b, n: (b, 0, n)),
            pl.BlockSpec((None, lc, g * hd), lambda b, n: (b, 0, ng + n)),
            pl.BlockSpec((l, hd), lambda b, n: (0, 0)),
            pl.BlockSpec((l, hd), lambda b, n: (0, 0)),
        ],
        out_specs=pl.BlockSpec((None, l, g * hd), lambda b, n: (b, 0, n)),
        out_shape=jax.ShapeDtypeStruct((nb, l, RET_W), BF16),
        scratch_shapes=[pltpu.VMEM((g, d, 4 * hd), BF16), pltpu.VMEM((g, l, 4 * hd), F32)],
        compiler_params=_params(2),
        name="retention",
    )(lgf, lgb, h, w_in, w_in, w_in, w_in, kv_ctx, kv_ctx, cos2, sin2)


CONV_COL_TILE = 256


def _conv_kernel(h_ref, wb_ref, wc_ref, wu_ref, wconv_ref, o_ref, w_s, p_s):
    tc = CONV_COL_TILE
    l = h_ref.shape[0]
    w_s[:, 0 * tc:1 * tc] = wb_ref[...].astype(BF16)
    w_s[:, 1 * tc:2 * tc] = wc_ref[...].astype(BF16)
    w_s[:, 2 * tc:3 * tc] = wu_ref[...].astype(BF16)
    p_s[...] = jnp.dot(h_ref[...], w_s[...], preferred_element_type=F32)
    z = p_s[:, 1 * tc:2 * tc] * p_s[:, 2 * tc:3 * tc]
    t = lax.broadcasted_iota(jnp.int32, (l, 1), 0)
    z_prev = jnp.where(t == 0, 0.0, pltpu.roll(z, 1, axis=0))
    z_next = jnp.where(t == l - 1, 0.0, pltpu.roll(z, l - 1, axis=0))
    conv = z_prev * wconv_ref[0:1, :] + z * wconv_ref[1:2, :] + z_next * wconv_ref[2:3, :]
    o_ref[...] = (p_s[:, 0 * tc:1 * tc] * conv).astype(BF16)


def _gated_conv(h, w_in, w_conv, e):
    nb, l, d = h.shape
    tc = CONV_COL_TILE
    nt = CONV_W // tc
    col0 = 4 * RET_W // tc

    def wspec(part):
        return pl.BlockSpec((None, d, tc), lambda b, n: (e, 0, col0 + part * nt + n))

    return pl.pallas_call(
        _conv_kernel,
        grid=(nb, nt),
        in_specs=[
            pl.BlockSpec((None, l, d), lambda b, n: (b, 0, 0)),
            wspec(0), wspec(1), wspec(2),
            pl.BlockSpec((None, 3, tc), lambda b, n: (e, 0, n)),
        ],
        out_specs=pl.BlockSpec((None, l, tc), lambda b, n: (b, 0, n)),
        out_shape=jax.ShapeDtypeStruct((nb, l, CONV_W), BF16),
        scratch_shapes=[pltpu.VMEM((d, 3 * tc), BF16), pltpu.VMEM((l, 3 * tc), F32)],
        compiler_params=_params(2),
        name="gated_conv",
    )(h, w_in, w_in, w_in, w_conv)


def _outproj_kernel(ar_ref, ac_ref, wr_ref, wc_ref, x_ref, mod_ref, o_ref):
    y = jnp.dot(ar_ref[...], wr_ref[...].astype(BF16), preferred_element_type=F32)
    y = y + jnp.dot(ac_ref[...], wc_ref[...].astype(BF16), preferred_element_type=F32)
    o_ref[...] = x_ref[...] + mod_ref[5:6, :] * y


def _outproj(a_ret, a_conv, w_out, e, x, mod):
    nb, l, d = x.shape
    tn = OUT_COL_TILE
    return pl.pallas_call(
        _outproj_kernel,
        grid=(nb, d // tn),
        in_specs=[
            pl.BlockSpec((None, l, RET_W), lambda b, n: (b, 0, 0)),
            pl.BlockSpec((None, l, CONV_W), lambda b, n: (b, 0, 0)),
            pl.BlockSpec((None, RET_W, tn), lambda b, n: (e, 0, n)),
            pl.BlockSpec((None, CONV_W, tn), lambda b, n: (e, 1, n)),
            pl.BlockSpec((None, l, tn), lambda b, n: (b, 0, n)),
            pl.BlockSpec((None, N_MOD, tn), lambda b, n: (b, 0, n)),
        ],
        out_specs=pl.BlockSpec((None, l, tn), lambda b, n: (b, 0, n)),
        out_shape=jax.ShapeDtypeStruct((nb, l, d), F32),
        compiler_params=_params(2),
        name="outproj",
    )(a_ret, a_conv, w_out, w_out, x, mod)


POOL_ROW_CHUNK = 256
POOL_HALO = 16


def _pool_kernel(h_ref, w_ref, ps_ref, x_ref, mod_ref, o_ref, hp_s, p_s):
    l, gw = h_ref.shape
    r, halo = POOL_ROW_CHUNK, POOL_HALO
    g = pl.program_id(1)
    lo = sum(jnp.where(g == gi, w // 2, 0) for gi, w in enumerate(POOL_WINDOWS))
    hi = sum(jnp.where(g == gi, w - w // 2, 0) for gi, w in enumerate(POOL_WINDOWS))

    hp_s[0:halo, :] = jnp.zeros((halo, gw), BF16)
    hp_s[halo + l:, :] = jnp.zeros((halo, gw), BF16)
    hp_s[halo:halo + l, :] = h_ref[...]

    ri = lax.broadcasted_iota(jnp.int32, (r, r + 2 * halo), 0)
    ci = lax.broadcasted_iota(jnp.int32, (r, r + 2 * halo), 1)
    off = ci - halo - ri
    band = jnp.where(off >= -lo, jnp.where(off < hi, 1.0, 0.0), 0.0).astype(BF16)
    for n in range(l // r):
        t = n * r + lax.broadcasted_iota(jnp.int32, (r, 1), 0)
        cnt = (jnp.minimum(t + hi, l) - jnp.maximum(t - lo, 0)).astype(F32)
        win = jnp.dot(band, hp_s[n * r:(n + 1) * r + 2 * halo, :], preferred_element_type=F32)
        hc = hp_s[n * r + halo:(n + 1) * r + halo, :].astype(F32)
        p_s[n * r:(n + 1) * r, :] = (win / cnt - hc).astype(BF16)

    y = jnp.dot(p_s[...], w_ref[...].astype(BF16), preferred_element_type=F32) * ps_ref[...]
    o_ref[...] = x_ref[...] + mod_ref[5:6, :] * y


def _pool(h, pool_w, pool_scale, o, x, mod):
    nb, l, d = x.shape
    gw = POOL_GROUP_W
    ng = d // gw
    assert max(POOL_WINDOWS) // 2 <= POOL_HALO and l % POOL_ROW_CHUNK == 0
    return pl.pallas_call(
        _pool_kernel,
        grid=(nb, ng),
        in_specs=[
            pl.BlockSpec((None, l, gw), lambda b, g: (b, 0, g)),
            pl.BlockSpec((None, None, gw, gw), lambda b, g: (o, g, 0, 0)),
            pl.BlockSpec((None, 1, gw), lambda b, g: (o, 0, g)),
            pl.BlockSpec((None, l, gw), lambda b, g: (b, 0, g)),
            pl.BlockSpec((None, N_MOD, gw), lambda b, g: (b, 0, g)),
        ],
        out_specs=pl.BlockSpec((None, l, gw), lambda b, g: (b, 0, g)),
        out_shape=jax.ShapeDtypeStruct((nb, l, d), F32),
        scratch_shapes=[pltpu.VMEM((l + 2 * POOL_HALO, gw), BF16), pltpu.VMEM((l, gw), BF16)],
        compiler_params=_params(2),
        name="pool",
    )(h, pool_w, pool_scale.reshape(-1, 1, d), x, mod)


def _rope_tables(l):
    quarter = RET_HEAD_DIM // 4
    inv = ROPE_BASE ** (-jnp.arange(quarter, dtype=F32) / quarter)
    rows = (jnp.arange(l) // GRID_W).astype(F32)
    cols = (jnp.arange(l) % GRID_W).astype(F32)
    ang = jnp.concatenate([rows[:, None] * inv, cols[:, None] * inv], axis=-1)
    cos, sin = jnp.cos(ang), jnp.sin(ang)
    return jnp.concatenate([cos, cos], axis=-1), jnp.concatenate([-sin, sin], axis=-1)


def kernel(x, c, ctx, c_ctx, w_mod, b_mod, norm_ffn1, norm_mix, norm_ffn2, ffn1_w_gate, ffn1_w_up, ffn1_w_down,
           ffn2_w_gate, ffn2_w_up, ffn2_w_down, mix_w_in, mix_w_conv, mix_w_out, ret_decay_fwd, ret_decay_bwd,
           pool_w, pool_scale, final_norm):
    b, l, d = x.shape
    depth = w_mod.shape[0]
    lc = ctx.shape[1]
    assert b + 1 <= 8 and d == D_MODEL
    last_even = ((depth - 1) // 2) * 2

    cond8 = jnp.zeros((8, d), F32).at[:b].set(c).at[b].set(c_ctx)
    mod_all = _adaln(cond8, w_mod, b_mod).reshape(depth, 8, N_MOD, d)
    cos2, sin2 = _rope_tables(l)

    xc = ctx.reshape(1, b * lc, d)
    h = None
    for li in range(depth):
        mod = mod_all[li, :b]
        mod_ctx = mod_all[li, b:b + 1]
        ctx_needed = li <= last_even
        ctx_full = li < last_even
        ffn1_w = (ffn1_w_gate, ffn1_w_up, ffn1_w_down)
        if ctx_needed:
            xc, hc, w16 = _ffn(xc, mod_ctx, norm_ffn1[li], ffn1_w, norm_mix[li], li=li, row0=0, post='mix')
            x, h, _ = _ffn(x, mod, norm_ffn1[li], w16, norm_mix[li], row0=0, post='mix')
        else:
            x, h = _ffn_stream(x, mod, norm_ffn1[li], ffn1_w, norm_mix[li], li=li, row0=0, post='mix', hn_buf=h)
        if li % 2 == 0:
            e = li // 2
            assert not ctx_full, "a context stream that is read again by a later layer is not implemented"
            lg_f = jax.nn.log_sigmoid(ret_decay_fwd[e])
            lg_b = jax.nn.log_sigmoid(ret_decay_bwd[e])
            kv_ctx = _ctx_kv(hc[0], mix_w_in, e).reshape(b, lc, 2 * RET_W)
            a_ret = _retention(h, mix_w_in, e, kv_ctx, lg_f, lg_b, cos2, sin2)
            a_conv = _gated_conv(h, mix_w_in, mix_w_conv, e)
            x = _outproj(a_ret, a_conv, mix_w_out, e, x, mod)
        else:
            x = _pool(h, pool_w, pool_scale, li // 2, x, mod)
        x, _ = _ffn_stream(x, mod, norm_ffn2[li], (ffn2_w_gate, ffn2_w_up, ffn2_w_down), final_norm,
                           li=li, row0=6, post=('final' if li == depth - 1 else None))
    return x
```

```python
import functools

import jax
import jax.numpy as jnp
from jax import lax
from jax.experimental import pallas as pl
from jax.experimental.pallas import tpu as pltpu

D_MODEL = 2048
GRID_W = 64
RET_HEADS = 8
RET_HEAD_DIM = 128
RET_W = RET_HEADS * RET_HEAD_DIM
K_SCALE = RET_HEAD_DIM ** -0.5
ROPE_BASE = 10000.0
CONV_W = D_MODEL - RET_W
POOL_WINDOWS = (2, 4, 8, 16)
POOL_GROUP_W = D_MODEL // len(POOL_WINDOWS)
N_MOD = 9
MACARON = 0.5
EPS = 1e-6
GN_EPS = 1e-5

F32 = jnp.float32
BF16 = jnp.bfloat16

V7X_VMEM_BYTES = 64 * 1024 * 1024
VMEM_LIMIT = V7X_VMEM_BYTES - 7 * 1024 * 1024

FFN_TOKEN_TILE = 1024
FFN_FF_TILE = 256
FFN_FF_TILE_BF16 = 512
FFN_ROW_CHUNK = 128
FFN_FIRST_ROW_BLOCK = 256
FFN_DOWN_CHUNK = 512
NORM_COL_BLOCK = 128
ADALN_COL_TILE = 1024
CTX_KV_COL_TILE = 256
RET_CHUNK = 256
RET_HEADS_PER_STEP = 2
OUT_COL_TILE = 512


def _params(n_axes, vmem=VMEM_LIMIT):
    return pltpu.CompilerParams(dimension_semantics=("arbitrary",) * n_axes, vmem_limit_bytes=vmem)


def _silu(v):
    return v / (1.0 + jnp.exp(-v))


def _bdot(a, b):
    return jnp.dot(a.astype(BF16), b.astype(BF16), preferred_element_type=F32)


def _adaln_kernel(cond_ref, w_ref, b_ref, o_ref):
    s = _silu(cond_ref[...])
    o_ref[...] = _bdot(s, w_ref[...]) + b_ref[...]


def _adaln(cond8, w_mod, b_mod):
    depth, d, n = w_mod.shape
    tn = ADALN_COL_TILE
    return pl.pallas_call(
        _adaln_kernel,
        grid=(depth, n // tn),
        in_specs=[
            pl.BlockSpec((8, d), lambda l, j: (0, 0)),
            pl.BlockSpec((None, d, tn), lambda l, j: (l, 0, j)),
            pl.BlockSpec((None, 1, tn), lambda l, j: (l, 0, j)),
        ],
        out_specs=pl.BlockSpec((None, 8, tn), lambda l, j: (l, 0, j)),
        out_shape=jax.ShapeDtypeStruct((depth, 8, n), F32),
        compiler_params=_params(2),
        name="adaln",
    )(cond8, w_mod, b_mod.reshape(depth, 1, n))


def _col_blocks(d):
    return [slice(c0, c0 + NORM_COL_BLOCK) for c0 in range(0, d, NORM_COL_BLOCK)]


def _rstd_from_partial(acc, d):
    return lax.rsqrt(jnp.sum(acc, axis=-1, keepdims=True) * (1.0 / d) + EPS)


def _ffn_kernel(*refs, row0, post, emit_w, n_in):
    x_ref, mod_ref, gn_ref, wg_ref, wu_ref, wd_ref, pn_ref = refs[:7]
    refs = list(refs[n_in:])
    o_ref = refs.pop(0)
    hn_ref = refs.pop(0) if post == 'mix' else None
    wg16_ref, wu16_ref, wd16_ref = (refs.pop(0), refs.pop(0), refs.pop(0)) if emit_w else (None, None, None)
    (h_ref,) = refs if post != 'mix' else (hn_ref,)
    j = pl.program_id(1)
    nj = pl.num_programs(1)
    tm, d = x_ref.shape
    rc = FFN_ROW_CHUNK
    blocks = _col_blocks(d)

    def row_chunk(r):
        return pl.ds(pl.multiple_of(r * rc, rc), rc)

    hids = [slice(f0, f0 + FFN_FF_TILE) for f0 in range(0, wg_ref.shape[1], FFN_FF_TILE)]

    def sub_tile(hid, rows, first):
        wg = wg_ref[:, hid].astype(BF16)
        wu = wu_ref[:, hid].astype(BF16)
        if emit_w:
            wg16_ref[:, hid] = wg
            wu16_ref[:, hid] = wu
        gate = jnp.dot(h_ref[rows, :], wg, preferred_element_type=F32)
        up = jnp.dot(h_ref[rows, :], wu, preferred_element_type=F32)
        a = (_silu(gate) * up).astype(BF16)
        for n0 in range(0, d, FFN_DOWN_CHUNK):
            cols = slice(n0, n0 + FFN_DOWN_CHUNK)
            wd = wd_ref[hid, cols].astype(BF16)
            if emit_w:
                wd16_ref[hid, cols] = wd
            y = jnp.dot(a, wd, preferred_element_type=F32)
            if first:
                o_ref[rows, cols] = y
            else:
                o_ref[rows, cols] += y

    @pl.when(j == 0)
    def _():
        shift = mod_ref[row0:row0 + 1, :]
        gs = gn_ref[...] * (1.0 + mod_ref[row0 + 1:row0 + 2, :])
        for r0 in range(0, tm, FFN_FIRST_ROW_BLOCK):
            for r1 in range(r0, r0 + FFN_FIRST_ROW_BLOCK, rc):
                rows = slice(r1, r1 + rc)
                acc = jnp.zeros((rc, NORM_COL_BLOCK), F32)
                for cols in blocks:
                    xb = x_ref[rows, cols]
                    acc = acc + xb * xb
                rstd = _rstd_from_partial(acc, d)
                for cols in blocks:
                    h_ref[rows, cols] = (x_ref[rows, cols] * rstd * gs[:, cols] + shift[:, cols]).astype(BF16)
            rows = slice(r0, r0 + FFN_FIRST_ROW_BLOCK)
            for hid in hids:
                sub_tile(hid, rows, first=hid is hids[0])

    @pl.when(j > 0)
    def _():
        for hid in hids:
            sub_tile(hid, slice(None), first=False)

    @pl.when(j == nj - 1)
    def _():
        res_gate = MACARON * mod_ref[row0 + 2:row0 + 3, :]
        pn = pn_ref[...]
        if post == 'mix':
            shift2 = mod_ref[3:4, :]
            gs2 = pn * (1.0 + mod_ref[4:5, :])

        def body(r, carry):
            rows = row_chunk(r)
            acc = jnp.zeros((rc, NORM_COL_BLOCK), F32)
            for cols in blocks:
                xn = x_ref[rows, cols] + res_gate[:, cols] * o_ref[rows, cols]
                o_ref[rows, cols] = xn
                if post is not None:
                    acc = acc + xn * xn
            if post is not None:
                rstd = _rstd_from_partial(acc, d)
                for cols in blocks:
                    if post == 'final':
                        o_ref[rows, cols] = o_ref[rows, cols] * rstd * pn[:, cols]
                    else:
                        hn_ref[rows, cols] = (o_ref[rows, cols] * rstd * gs2[:, cols] + shift2[:, cols]).astype(BF16)
            return carry

        lax.fori_loop(0, tm // rc, body, 0)


def _ffn(x, mod, g_norm, weights, post_norm, *, row0, post, tiles=None, li=None, hn_buf=None):
    nb, t, d = x.shape
    tm = min(FFN_TOKEN_TILE, t)
    tpb = t // tm
    emit_w = li is not None
    inplace = tiles is not None
    t0, nt = tiles if inplace else (0, nb * tpb)
    w_gate, w_up, w_down = weights
    d_ff = w_gate.shape[-1]
    tf = FFN_FF_TILE if emit_w else FFN_FF_TILE_BF16

    def rows(n, j):
        return ((t0 + n) // tpb, (t0 + n) % tpb, 0)

    if emit_w:
        w_specs = [pl.BlockSpec((None, d, tf), lambda n, j: (li, 0, j)),
                   pl.BlockSpec((None, d, tf), lambda n, j: (li, 0, j)),
                   pl.BlockSpec((None, tf, d), lambda n, j: (li, j, 0))]
    else:
        w_specs = [pl.BlockSpec((d, tf), lambda n, j: (0, j)),
                   pl.BlockSpec((d, tf), lambda n, j: (0, j)),
                   pl.BlockSpec((tf, d), lambda n, j: (j, 0))]
    in_specs = [pl.BlockSpec((None, tm, d), rows),
                pl.BlockSpec((None, N_MOD, d), lambda n, j: ((t0 + n) // tpb, 0, 0)),
                pl.BlockSpec((1, d), lambda n, j: (0, 0)),
                *w_specs,
                pl.BlockSpec((1, d), lambda n, j: (0, 0))]
    args = [x, mod, g_norm.reshape(1, d), w_gate, w_up, w_down, post_norm.reshape(1, d)]
    out_specs = [pl.BlockSpec((None, tm, d), rows)]
    out_shape = [jax.ShapeDtypeStruct((nb, t, d), F32)]
    aliases = {0: 0} if inplace else {}
    if post == 'mix':
        out_specs.append(pl.BlockSpec((None, tm, d), rows))
        out_shape.append(jax.ShapeDtypeStruct((nb, t, d), BF16))
        if inplace:
            in_specs.append(pl.BlockSpec(memory_space=pl.ANY))
            args.append(hn_buf)
            aliases[len(args) - 1] = 1
    if emit_w:
        out_specs += [pl.BlockSpec((d, tf), lambda n, j: (0, j)),
                      pl.BlockSpec((d, tf), lambda n, j: (0, j)),
                      pl.BlockSpec((tf, d), lambda n, j: (j, 0))]
        out_shape += [jax.ShapeDtypeStruct((d, d_ff), BF16), jax.ShapeDtypeStruct((d, d_ff), BF16),
                      jax.ShapeDtypeStruct((d_ff, d), BF16)]
        assert nt == 1, "the bf16 weight casts are emitted once, by a single-tile call"
    out = pl.pallas_call(
        functools.partial(_ffn_kernel, row0=row0, post=post, emit_w=emit_w, n_in=len(args)),
        grid=(nt, d_ff // tf),
        in_specs=in_specs,
        out_specs=out_specs,
        out_shape=out_shape,
        input_output_aliases=aliases,
        scratch_shapes=[] if post == 'mix' else [pltpu.VMEM((tm, d), BF16)],
        compiler_params=_params(2),
        name="ffn",
    )(*args)
    hn = out[1] if post == 'mix' else None
    w16 = tuple(out[-3:]) if emit_w else None
    return out[0], hn, w16


def _ffn_stream(x, mod, g_norm, weights, post_norm, *, li, row0, post, hn_buf=None):
    nb, t, d = x.shape
    n_tiles = nb * (t // min(FFN_TOKEN_TILE, t))
    if post == 'mix' and hn_buf is None:
        hn_buf = jnp.zeros((nb, t, d), BF16)
    x, hn, w16 = _ffn(x, mod, g_norm, weights, post_norm, row0=row0, post=post, tiles=(0, 1), li=li, hn_buf=hn_buf)
    if n_tiles > 1:
        x, hn, _ = _ffn(x, mod, g_norm, w16, post_norm, row0=row0, post=post, tiles=(1, n_tiles - 1), hn_buf=hn)
    return x, hn


def _ctx_kv_kernel(h_ref, w_ref, o_ref):
    o_ref[...] = jnp.dot(h_ref[...], w_ref[...].astype(BF16), preferred_element_type=F32)


def _ctx_kv(hc, w_in, e):
    t, d = hc.shape
    tn = CTX_KV_COL_TILE
    col0 = RET_W // tn
    return pl.pallas_call(
        _ctx_kv_kernel,
        grid=(2 * RET_W // tn,),
        in_specs=[
            pl.BlockSpec((t, d), lambda j: (0, 0)),
            pl.BlockSpec((None, d, tn), lambda j: (e, 0, col0 + j)),
        ],
        out_specs=pl.BlockSpec((t, tn), lambda j: (0, j)),
        out_shape=jax.ShapeDtypeStruct((t, 2 * RET_W), F32),
        compiler_params=_params(1),
        name="ctx_kv",
    )(hc, w_in)


def _ret_kernel(lgf_ref, lgb_ref, h_ref, wq_ref, wk_ref, wv_ref, wg_ref, kc_ref, vc_ref,
                cos_ref, sin_ref, o_ref, w_s, p_s):
    hd = RET_HEAD_DIM
    for t in range(RET_HEADS_PER_STEP):
        cols = slice(t * hd, (t + 1) * hd)
        w_s[t, :, 0 * hd:1 * hd] = wq_ref[:, cols].astype(BF16)
        w_s[t, :, 1 * hd:2 * hd] = wk_ref[:, cols].astype(BF16)
        w_s[t, :, 2 * hd:3 * hd] = wv_ref[:, cols].astype(BF16)
        w_s[t, :, 3 * hd:4 * hd] = wg_ref[:, cols].astype(BF16)
        p_s[t] = jnp.dot(h_ref[...], w_s[t], preferred_element_type=F32)
    for t in range(RET_HEADS_PER_STEP):
        cols = slice(t * hd, (t + 1) * hd)
        _ret_head(lgf_ref[t, 0:1, 0:1], lgb_ref[t, 0:1, 0:1], p_s.at[t], kc_ref[:, cols], vc_ref[:, cols],
                  cos_ref[...], sin_ref[...], o_ref, cols)


def _ret_head(lgf, lgb, p_s, kc, vcx, cos2, sin2, o_ref, out_cols):
    hd = RET_HEAD_DIM
    l = p_s.shape[0]
    lc = kc.shape[0]
    c = RET_CHUNK
    n_chunks = l // c

    q = p_s[:, 0 * hd:1 * hd]
    p_s[:, 0 * hd:1 * hd] = q * cos2 + pltpu.roll(q, hd // 2, axis=1) * sin2
    k = p_s[:, 1 * hd:2 * hd] * K_SCALE
    p_s[:, 1 * hd:2 * hd] = k * cos2 + pltpu.roll(k, hd // 2, axis=1) * sin2

    pos_c = lax.broadcasted_iota(jnp.int32, (1, lc), 1).astype(F32)
    kct = (kc * K_SCALE).T
    s_fwd = _bdot(kct * jnp.exp(lgf * (lc - 1.0 - pos_c)), vcx)
    s_bwd = _bdot(kct * jnp.exp(lgb * pos_c), vcx)

    pos = lax.broadcasted_iota(jnp.int32, (1, c), 1).astype(F32)
    wf_row = jnp.exp(lgf * (c - 1.0 - pos))
    wb_row = jnp.exp(lgb * pos)
    kv_f, kv_b = [], []
    for n in range(n_chunks):
        rows = slice(n * c, (n + 1) * c)
        kt = p_s[rows, 1 * hd:2 * hd].T
        kw = jnp.concatenate([kt * wf_row, kt * wb_row], axis=0)
        kv = _bdot(kw, p_s[rows, 2 * hd:3 * hd])
        kv_f.append(kv[:hd])
        kv_b.append(kv[hd:])

    gfc = jnp.exp(lgf * float(c))
    gbc = jnp.exp(lgb * float(c))
    st_f = [s_fwd]
    for n in range(n_chunks - 1):
        st_f.append(st_f[-1] * gfc + kv_f[n])
    st_b = [s_bwd]
    for n in range(n_chunks - 1, 0, -1):
        st_b.append(st_b[-1] * gbc + kv_b[n])
    st_b = st_b[::-1]

    ri = lax.broadcasted_iota(jnp.int32, (c, c), 0)
    ci = lax.broadcasted_iota(jnp.int32, (c, c), 1)
    diff = (ri - ci).astype(F32)
    decay = jnp.where(diff > 0, jnp.exp(lgf * jnp.maximum(diff, 0.0)),
                      jnp.where(diff < 0, jnp.exp(lgb * jnp.maximum(-diff, 0.0)), 2.0))
    rpos = lax.broadcasted_iota(jnp.int32, (c, 1), 0).astype(F32)
    qf_col = jnp.exp(lgf * (rpos + 1.0))
    qb_col = jnp.exp(lgb * (c - rpos))

    for n in range(n_chunks):
        rows = slice(n * c, (n + 1) * c)
        qn = p_s[rows, 0 * hd:1 * hd]
        kn = p_s[rows, 1 * hd:2 * hd]
        vn = p_s[rows, 2 * hd:3 * hd]
        gn = p_s[rows, 3 * hd:4 * hd]
        scores = lax.dot_general(qn.astype(BF16), kn.astype(BF16), (((1,), (1,)), ((), ())),
                                 preferred_element_type=F32)
        o = _bdot(scores * decay, vn)
        qd = jnp.concatenate([qn * qf_col, qn * qb_col], axis=1)
        st = jnp.concatenate([st_f[n], st_b[n]], axis=0)
        o = o + _bdot(qd, st)
        mu = jnp.mean(o, axis=-1, keepdims=True)
        oc = o - mu
        var = jnp.mean(oc * oc, axis=-1, keepdims=True)
        o_ref[rows, out_cols] = (oc * lax.rsqrt(var + GN_EPS) * _silu(gn)).astype(BF16)


def _retention(h, w_in, e, kv_ctx, lg_f, lg_b, cos2, sin2):
    nb, l, d = h.shape
    lc = kv_ctx.shape[1]
    hd = RET_HEAD_DIM
    g = RET_HEADS_PER_STEP
    ng = RET_HEADS // g
    lgf = jnp.broadcast_to(lg_f.astype(F32)[:, None, None], (RET_HEADS, 8, 128))
    lgb = jnp.broadcast_to(lg_b.astype(F32)[:, None, None], (RET_HEADS, 8, 128))

    def wspec(part):
        return pl.BlockSpec((None, d, g * hd), lambda b, n: (e, 0, part * ng + n))

    return pl.pallas_call(
        _ret_kernel,
        grid=(nb, ng),
        in_specs=[
            pl.BlockSpec((g, 8, 128), lambda b, n: (n, 0, 0)),
            pl.BlockSpec((g, 8, 128), lambda b, n: (n, 0, 0)),
            pl.BlockSpec((None, l, d), lambda b, n: (b, 0, 0)),
            wspec(0), wspec(1), wspec(2), wspec(3),
            pl.BlockSpec((None, lc, g * hd), lambda b, n: (b, 0, n)),
            pl.BlockSpec((None, lc, g * hd), lambda b, n: (b, 0, ng + n)),
            pl.BlockSpec((l, hd), lambda b, n: (0, 0)),
            pl.BlockSpec((l, hd), lambda b, n: (0, 0)),
        ],
        out_specs=pl.BlockSpec((None, l, g * hd), lambda b, n: (b, 0, n)),
        out_shape=jax.ShapeDtypeStruct((nb, l, RET_W), BF16),
        scratch_shapes=[pltpu.VMEM((g, d, 4 * hd), BF16), pltpu.VMEM((g, l, 4 * hd), F32)],
        compiler_params=_params(2),
        name="retention",
    )(lgf, lgb, h, w_in, w_in, w_in, w_in, kv_ctx, kv_ctx, cos2, sin2)


CONV_COL_TILE = 256


def _conv_kernel(h_ref, wb_ref, wc_ref, wu_ref, wconv_ref, o_ref, w_s, p_s):
    tc = CONV_COL_TILE
    l = h_ref.shape[0]
    w_s[:, 0 * tc:1 * tc] = wb_ref[...].astype(BF16)
    w_s[:, 1 * tc:2 * tc] = wc_ref[...].astype(BF16)
    w_s[:, 2 * tc:3 * tc] = wu_ref[...].astype(BF16)
    p_s[...] = jnp.dot(h_ref[...], w_s[...], preferred_element_type=F32)
    z = p_s[:, 1 * tc:2 * tc] * p_s[:, 2 * tc:3 * tc]
    t = lax.broadcasted_iota(jnp.int32, (l, 1), 0)
    z_prev = jnp.where(t == 0, 0.0, pltpu.roll(z, 1, axis=0))
    z_next = jnp.where(t == l - 1, 0.0, pltpu.roll(z, l - 1, axis=0))
    conv = z_prev * wconv_ref[0:1, :] + z * wconv_ref[1:2, :] + z_next * wconv_ref[2:3, :]
    o_ref[...] = (p_s[:, 0 * tc:1 * tc] * conv).astype(BF16)


def _gated_conv(h, w_in, w_conv, e):
    nb, l, d = h.shape
    tc = CONV_COL_TILE
    nt = CONV_W // tc
    col0 = 4 * RET_W // tc

    def wspec(part):
        return pl.BlockSpec((None, d, tc), lambda b, n: (e, 0, col0 + part * nt + n))

    return pl.pallas_call(
        _conv_kernel,
        grid=(nb, nt),
        in_specs=[
            pl.BlockSpec((None, l, d), lambda b, n: (b, 0, 0)),
            wspec(0), wspec(1), wspec(2),
            pl.BlockSpec((None, 3, tc), lambda b, n: (e, 0, n)),
        ],
        out_specs=pl.BlockSpec((None, l, tc), lambda b, n: (b, 0, n)),
        out_shape=jax.ShapeDtypeStruct((nb, l, CONV_W), BF16),
        scratch_shapes=[pltpu.VMEM((d, 3 * tc), BF16), pltpu.VMEM((l, 3 * tc), F32)],
        compiler_params=_params(2),
        name="gated_conv",
    )(h, w_in, w_in, w_in, w_conv)


def _outproj_kernel(ar_ref, ac_ref, wr_ref, wc_ref, x_ref, mod_ref, o_ref):
    y = jnp.dot(ar_ref[...], wr_ref[...].astype(BF16), preferred_element_type=F32)
    y = y + jnp.dot(ac_ref[...], wc_ref[...].astype(BF16), preferred_element_type=F32)
    o_ref[...] = x_ref[...] + mod_ref[5:6, :] * y


def _outproj(a_ret, a_conv, w_out, e, x, mod):
    nb, l, d = x.shape
    tn = OUT_COL_TILE
    return pl.pallas_call(
        _outproj_kernel,
        grid=(nb, d // tn),
        in_specs=[
            pl.BlockSpec((None, l, RET_W), lambda b, n: (b, 0, 0)),
            pl.BlockSpec((None, l, CONV_W), lambda b, n: (b, 0, 0)),
            pl.BlockSpec((None, RET_W, tn), lambda b, n: (e, 0, n)),
            pl.BlockSpec((None, CONV_W, tn), lambda b, n: (e, 1, n)),
            pl.BlockSpec((None, l, tn), lambda b, n: (b, 0, n)),
            pl.BlockSpec((None, N_MOD, tn), lambda b, n: (b, 0, n)),
        ],
        out_specs=pl.BlockSpec((None, l, tn), lambda b, n: (b, 0, n)),
        out_shape=jax.ShapeDtypeStruct((nb, l, d), F32),
        compiler_params=_params(2),
        name="outproj",
    )(a_ret, a_conv, w_out, w_out, x, mod)


POOL_ROW_CHUNK = 256
POOL_HALO = 16


def _pool_kernel(h_ref, w_ref, ps_ref, x_ref, mod_ref, o_ref, hp_s, p_s):
    l, gw = h_ref.shape
    r, halo = POOL_ROW_CHUNK, POOL_HALO
    g = pl.program_id(1)
    lo = sum(jnp.where(g == gi, w // 2, 0) for gi, w in enumerate(POOL_WINDOWS))
    hi = sum(jnp.where(g == gi, w - w // 2, 0) for gi, w in enumerate(POOL_WINDOWS))

    hp_s[0:halo, :] = jnp.zeros((halo, gw), BF16)
    hp_s[halo + l:, :] = jnp.zeros((halo, gw), BF16)
    hp_s[halo:halo + l, :] = h_ref[...]

    ri = lax.broadcasted_iota(jnp.int32, (r, r + 2 * halo), 0)
    ci = lax.broadcasted_iota(jnp.int32, (r, r + 2 * halo), 1)
    off = ci - halo - ri
    band = jnp.where(off >= -lo, jnp.where(off < hi, 1.0, 0.0), 0.0).astype(BF16)
    for n in range(l // r):
        t = n * r + lax.broadcasted_iota(jnp.int32, (r, 1), 0)
        cnt = (jnp.minimum(t + hi, l) - jnp.maximum(t - lo, 0)).astype(F32)
        win = jnp.dot(band, hp_s[n * r:(n + 1) * r + 2 * halo, :], preferred_element_type=F32)
        hc = hp_s[n * r + halo:(n + 1) * r + halo, :].astype(F32)
        p_s[n * r:(n + 1) * r, :] = (win / cnt - hc).astype(BF16)

    y = jnp.dot(p_s[...], w_ref[...].astype(BF16), preferred_element_type=F32) * ps_ref[...]
    o_ref[...] = x_ref[...] + mod_ref[5:6, :] * y


def _pool(h, pool_w, pool_scale, o, x, mod):
    nb, l, d = x.shape
    gw = POOL_GROUP_W
    ng = d // gw
    assert max(POOL_WINDOWS) // 2 <= POOL_HALO and l % POOL_ROW_CHUNK == 0
    return pl.pallas_call(
        _pool_kernel,
        grid=(nb, ng),
        in_specs=[
            pl.BlockSpec((None, l, gw), lambda b, g: (b, 0, g)),
            pl.BlockSpec((None, None, gw, gw), lambda b, g: (o, g, 0, 0)),
            pl.BlockSpec((None, 1, gw), lambda b, g: (o, 0, g)),
            pl.BlockSpec((None, l, gw), lambda b, g: (b, 0, g)),
            pl.BlockSpec((None, N_MOD, gw), lambda b, g: (b, 0, g)),
        ],
        out_specs=pl.BlockSpec((None, l, gw), lambda b, g: (b, 0, g)),
        out_shape=jax.ShapeDtypeStruct((nb, l, d), F32),
        scratch_shapes=[pltpu.VMEM((l + 2 * POOL_HALO, gw), BF16), pltpu.VMEM((l, gw), BF16)],
        compiler_params=_params(2),
        name="pool",
    )(h, pool_w, pool_scale.reshape(-1, 1, d), x, mod)


def _rope_tables(l):
    quarter = RET_HEAD_DIM // 4
    inv = ROPE_BASE ** (-jnp.arange(quarter, dtype=F32) / quarter)
    rows = (jnp.arange(l) // GRID_W).astype(F32)
    cols = (jnp.arange(l) % GRID_W).astype(F32)
    ang = jnp.concatenate([rows[:, None] * inv, cols[:, None] * inv], axis=-1)
    cos, sin = jnp.cos(ang), jnp.sin(ang)
    return jnp.concatenate([cos, cos], axis=-1), jnp.concatenate([-sin, sin], axis=-1)


def kernel(x, c, ctx, c_ctx, w_mod, b_mod, norm_ffn1, norm_mix, norm_ffn2, ffn1_w_gate, ffn1_w_up, ffn1_w_down,
           ffn2_w_gate, ffn2_w_up, ffn2_w_down, mix_w_in, mix_w_conv, mix_w_out, ret_decay_fwd, ret_decay_bwd,
           pool_w, pool_scale, final_norm):
    b, l, d = x.shape
    depth = w_mod.shape[0]
    lc = ctx.shape[1]
    assert b + 1 <= 8 and d == D_MODEL
    last_even = ((depth - 1) // 2) * 2

    cond8 = jnp.zeros((8, d), F32).at[:b].set(c).at[b].set(c_ctx)
    mod_all = _adaln(cond8, w_mod, b_mod).reshape(depth, 8, N_MOD, d)
    cos2, sin2 = _rope_tables(l)

    xc = ctx.reshape(1, b * lc, d)
    h = None
    for li in range(depth):
        mod = mod_all[li, :b]
        mod_ctx = mod_all[li, b:b + 1]
        ctx_needed = li <= last_even
        ctx_full = li < last_even
        ffn1_w = (ffn1_w_gate, ffn1_w_up, ffn1_w_down)
        if ctx_needed:
            xc, hc, w16 = _ffn(xc, mod_ctx, norm_ffn1[li], ffn1_w, norm_mix[li], li=li, row0=0, post='mix')
            x, h, _ = _ffn(x, mod, norm_ffn1[li], w16, norm_mix[li], row0=0, post='mix')
        else:
            x, h = _ffn_stream(x, mod, norm_ffn1[li], ffn1_w, norm_mix[li], li=li, row0=0, post='mix', hn_buf=h)
        if li % 2 == 0:
            e = li // 2
            assert not ctx_full, "a context stream that is read again by a later layer is not implemented"
            lg_f = jax.nn.log_sigmoid(ret_decay_fwd[e])
            lg_b = jax.nn.log_sigmoid(ret_decay_bwd[e])
            kv_ctx = _ctx_kv(hc[0], mix_w_in, e).reshape(b, lc, 2 * RET_W)
            a_ret = _retention(h, mix_w_in, e, kv_ctx, lg_f, lg_b, cos2, sin2)
            a_conv = _gated_conv(h, mix_w_in, mix_w_conv, e)
            x = _outproj(a_ret, a_conv, mix_w_out, e, x, mod)
        else:
            x = _pool(h, pool_w, pool_scale, li // 2, x, mod)
        x, _ = _ffn_stream(x, mod, norm_ffn2[li], (ffn2_w_gate, ffn2_w_up, ffn2_w_down), final_norm,
                           li=li, row0=6, post=('final' if li == depth - 1 else None))
    return x
```

```python
import functools

import jax
import jax.numpy as jnp
from jax import lax
from jax.experimental import pallas as pl
from jax.experimental.pallas import tpu as pltpu

D_MODEL = 2048
GRID_W = 64
RET_HEADS = 8
RET_HEAD_DIM = 128
RET_W = RET_HEADS * RET_HEAD_DIM
K_SCALE = RET_HEAD_DIM ** -0.5
ROPE_BASE = 10000.0
CONV_W = D_MODEL - RET_W
POOL_WINDOWS = (2, 4, 8, 16)
POOL_GROUP_W = D_MODEL // len(POOL_WINDOWS)
N_MOD = 9
MACARON = 0.5
EPS = 1e-6
GN_EPS = 1e-5

F32 = jnp.float32
BF16 = jnp.bfloat16

V7X_VMEM_BYTES = 64 * 1024 * 1024
VMEM_LIMIT = V7X_VMEM_BYTES - 7 * 1024 * 1024

FFN_TOKEN_TILE = 1024
FFN_FF_TILE = 256
FFN_FF_TILE_BF16 = 512
FFN_ROW_CHUNK = 128
FFN_FIRST_ROW_BLOCK = 512
FFN_DOWN_CHUNK = 512
NORM_COL_BLOCK = 128
ADALN_COL_TILE = 1024
CTX_KV_COL_TILE = 256
RET_CHUNK = 256
RET_HEADS_PER_STEP = 2
OUT_COL_TILE = 512


def _params(n_axes, vmem=VMEM_LIMIT):
    return pltpu.CompilerParams(dimension_semantics=("arbitrary",) * n_axes, vmem_limit_bytes=vmem)


def _silu(v):
    return v / (1.0 + jnp.exp(-v))


def _bdot(a, b):
    return jnp.dot(a.astype(BF16), b.astype(BF16), preferred_element_type=F32)


def _adaln_kernel(cond_ref, w_ref, b_ref, o_ref):
    s = _silu(cond_ref[...])
    o_ref[...] = _bdot(s, w_ref[...]) + b_ref[...]


def _adaln(cond8, w_mod, b_mod):
    depth, d, n = w_mod.shape
    tn = ADALN_COL_TILE
    return pl.pallas_call(
        _adaln_kernel,
        grid=(depth, n // tn),
        in_specs=[
            pl.BlockSpec((8, d), lambda l, j: (0, 0)),
            pl.BlockSpec((None, d, tn), lambda l, j: (l, 0, j)),
            pl.BlockSpec((None, 1, tn), lambda l, j: (l, 0, j)),
        ],
        out_specs=pl.BlockSpec((None, 8, tn), lambda l, j: (l, 0, j)),
        out_shape=jax.ShapeDtypeStruct((depth, 8, n), F32),
        compiler_params=_params(2),
        name="adaln",
    )(cond8, w_mod, b_mod.reshape(depth, 1, n))


def _col_blocks(d):
    return [slice(c0, c0 + NORM_COL_BLOCK) for c0 in range(0, d, NORM_COL_BLOCK)]


def _rstd_from_partial(acc, d):
    return lax.rsqrt(jnp.sum(acc, axis=-1, keepdims=True) * (1.0 / d) + EPS)


def _ffn_kernel(*refs, row0, post, emit_w, n_in):
    x_ref, mod_ref, gn_ref, wg_ref, wu_ref, wd_ref, pn_ref = refs[:7]
    refs = list(refs[n_in:])
    o_ref = refs.pop(0)
    hn_ref = refs.pop(0) if post == 'mix' else None
    wg16_ref, wu16_ref, wd16_ref = (refs.pop(0), refs.pop(0), refs.pop(0)) if emit_w else (None, None, None)
    (h_ref,) = refs if post != 'mix' else (hn_ref,)
    j = pl.program_id(1)
    nj = pl.num_programs(1)
    tm, d = x_ref.shape
    rc = FFN_ROW_CHUNK
    blocks = _col_blocks(d)

    def row_chunk(r):
        return pl.ds(pl.multiple_of(r * rc, rc), rc)

    hids = [slice(f0, f0 + FFN_FF_TILE) for f0 in range(0, wg_ref.shape[1], FFN_FF_TILE)]

    def sub_tile(hid, rows, first):
        wg = wg_ref[:, hid].astype(BF16)
        wu = wu_ref[:, hid].astype(BF16)
        if emit_w:
            wg16_ref[:, hid] = wg
            wu16_ref[:, hid] = wu
        gate = jnp.dot(h_ref[rows, :], wg, preferred_element_type=F32)
        up = jnp.dot(h_ref[rows, :], wu, preferred_element_type=F32)
        a = (_silu(gate) * up).astype(BF16)
        for n0 in range(0, d, FFN_DOWN_CHUNK):
            cols = slice(n0, n0 + FFN_DOWN_CHUNK)
            wd = wd_ref[hid, cols].astype(BF16)
            if emit_w:
                wd16_ref[hid, cols] = wd
            y = jnp.dot(a, wd, preferred_element_type=F32)
            if first:
                o_ref[rows, cols] = y
            else:
                o_ref[rows, cols] += y

    @pl.when(j == 0)
    def _():
        shift = mod_ref[row0:row0 + 1, :]
        gs = gn_ref[...] * (1.0 + mod_ref[row0 + 1:row0 + 2, :])
        for r0 in range(0, tm, FFN_FIRST_ROW_BLOCK):
            r_end = r0 + FFN_FIRST_ROW_BLOCK
            for r1 in range(r0, r_end, rc):
                rows = slice(r1, r1 + rc)
                acc = jnp.zeros((rc, NORM_COL_BLOCK), F32)
                for cols in blocks:
                    xb = x_ref[rows, cols]
                    acc = acc + xb * xb
                rstd = _rstd_from_partial(acc, d)
                for cols in blocks:
                    h_ref[rows, cols] = (x_ref[rows, cols] * rstd * gs[:, cols] + shift[:, cols]).astype(BF16)
            rows = slice(r0, r_end)
            for hid in hids:
                sub_tile(hid, rows, first=hid is hids[0])

    @pl.when(jnp.logical_and(j > 0, j < nj - 1))
    def _():
        for hid in hids:
            sub_tile(hid, slice(None), first=False)

    @pl.when(j == nj - 1)
    def _():
        res_gate = MACARON * mod_ref[row0 + 2:row0 + 3, :]
        pn = pn_ref[...]
        if post == 'mix':
            shift2 = mod_ref[3:4, :]
            gs2 = pn * (1.0 + mod_ref[4:5, :])
        for r0 in range(0, tm, FFN_FIRST_ROW_BLOCK):
            for hid in hids:
                sub_tile(hid, slice(r0, r0 + FFN_FIRST_ROW_BLOCK), first=False)
            for r1 in range(r0, r0 + FFN_FIRST_ROW_BLOCK, rc):
                rows = slice(r1, r1 + rc)
                acc = jnp.zeros((rc, NORM_COL_BLOCK), F32)
                for cols in blocks:
                    xn = x_ref[rows, cols] + res_gate[:, cols] * o_ref[rows, cols]
                    o_ref[rows, cols] = xn
                    if post is not None:
                        acc = acc + xn * xn
                if post is not None:
                    rstd = _rstd_from_partial(acc, d)
                    for cols in blocks:
                        if post == 'final':
                            o_ref[rows, cols] = o_ref[rows, cols] * rstd * pn[:, cols]
                        else:
                            hn_ref[rows, cols] = (o_ref[rows, cols] * rstd * gs2[:, cols]
                                                  + shift2[:, cols]).astype(BF16)


def _ffn(x, mod, g_norm, weights, post_norm, *, row0, post, tiles=None, li=None, hn_buf=None):
    nb, t, d = x.shape
    tm = min(FFN_TOKEN_TILE, t)
    tpb = t // tm
    emit_w = li is not None
    inplace = tiles is not None
    t0, nt = tiles if inplace else (0, nb * tpb)
    w_gate, w_up, w_down = weights
    d_ff = w_gate.shape[-1]
    tf = FFN_FF_TILE if emit_w else FFN_FF_TILE_BF16

    def rows(n, j):
        return ((t0 + n) // tpb, (t0 + n) % tpb, 0)

    if emit_w:
        w_specs = [pl.BlockSpec((None, d, tf), lambda n, j: (li, 0, j)),
                   pl.BlockSpec((None, d, tf), lambda n, j: (li, 0, j)),
                   pl.BlockSpec((None, tf, d), lambda n, j: (li, j, 0))]
    else:
        w_specs = [pl.BlockSpec((d, tf), lambda n, j: (0, j)),
                   pl.BlockSpec((d, tf), lambda n, j: (0, j)),
                   pl.BlockSpec((tf, d), lambda n, j: (j, 0))]
    in_specs = [pl.BlockSpec((None, tm, d), rows),
                pl.BlockSpec((None, N_MOD, d), lambda n, j: ((t0 + n) // tpb, 0, 0)),
                pl.BlockSpec((1, d), lambda n, j: (0, 0)),
                *w_specs,
                pl.BlockSpec((1, d), lambda n, j: (0, 0))]
    args = [x, mod, g_norm.reshape(1, d), w_gate, w_up, w_down, post_norm.reshape(1, d)]
    out_specs = [pl.BlockSpec((None, tm, d), rows)]
    out_shape = [jax.ShapeDtypeStruct((nb, t, d), F32)]
    aliases = {0: 0} if inplace else {}
    if post == 'mix':
        out_specs.append(pl.BlockSpec((None, tm, d), rows))
        out_shape.append(jax.ShapeDtypeStruct((nb, t, d), BF16))
        if inplace:
            in_specs.append(pl.BlockSpec(memory_space=pl.ANY))
            args.append(hn_buf)
            aliases[len(args) - 1] = 1
    if emit_w:
        out_specs += [pl.BlockSpec((d, tf), lambda n, j: (0, j)),
                      pl.BlockSpec((d, tf), lambda n, j: (0, j)),
                      pl.BlockSpec((tf, d), lambda n, j: (j, 0))]
        out_shape += [jax.ShapeDtypeStruct((d, d_ff), BF16), jax.ShapeDtypeStruct((d, d_ff), BF16),
                      jax.ShapeDtypeStruct((d_ff, d), BF16)]
        assert nt == 1, "the bf16 weight casts are emitted once, by a single-tile call"
    out = pl.pallas_call(
        functools.partial(_ffn_kernel, row0=row0, post=post, emit_w=emit_w, n_in=len(args)),
        grid=(nt, d_ff // tf),
        in_specs=in_specs,
        out_specs=out_specs,
        out_shape=out_shape,
        input_output_aliases=aliases,
        scratch_shapes=[] if post == 'mix' else [pltpu.VMEM((tm, d), BF16)],
        compiler_params=_params(2),
        name="ffn",
    )(*args)
    hn = out[1] if post == 'mix' else None
    w16 = tuple(out[-3:]) if emit_w else None
    return out[0], hn, w16


def _ffn_stream(x, mod, g_norm, weights, post_norm, *, li, row0, post, hn_buf=None):
    nb, t, d = x.shape
    n_tiles = nb * (t // min(FFN_TOKEN_TILE, t))
    if post == 'mix' and hn_buf is None:
        hn_buf = jnp.zeros((nb, t, d), BF16)
    x, hn, w16 = _ffn(x, mod, g_norm, weights, post_norm, row0=row0, post=post, tiles=(0, 1), li=li, hn_buf=hn_buf)
    if n_tiles > 1:
        x, hn, _ = _ffn(x, mod, g_norm, w16, post_norm, row0=row0, post=post, tiles=(1, n_tiles - 1), hn_buf=hn)
    return x, hn


def _ctx_kv_kernel(h_ref, w_ref, o_ref):
    o_ref[...] = jnp.dot(h_ref[...], w_ref[...].astype(BF16), preferred_element_type=F32)


def _ctx_kv(hc, w_in, e):
    t, d = hc.shape
    tn = CTX_KV_COL_TILE
    col0 = RET_W // tn
    return pl.pallas_call(
        _ctx_kv_kernel,
        grid=(2 * RET_W // tn,),
        in_specs=[
            pl.BlockSpec((t, d), lambda j: (0, 0)),
            pl.BlockSpec((None, d, tn), lambda j: (e, 0, col0 + j)),
        ],
        out_specs=pl.BlockSpec((t, tn), lambda j: (0, j)),
        out_shape=jax.ShapeDtypeStruct((t, 2 * RET_W), F32),
        compiler_params=_params(1),
        name="ctx_kv",
    )(hc, w_in)


def _ret_kernel(lgf_ref, lgb_ref, h_ref, wq_ref, wk_ref, wv_ref, wg_ref, kc_ref, vc_ref,
                cos_ref, sin_ref, o_ref, w_s, p_s):
    hd = RET_HEAD_DIM
    for t in range(RET_HEADS_PER_STEP):
        cols = slice(t * hd, (t + 1) * hd)
        w_s[t, :, 0 * hd:1 * hd] = wq_ref[:, cols].astype(BF16)
        w_s[t, :, 1 * hd:2 * hd] = wk_ref[:, cols].astype(BF16)
        w_s[t, :, 2 * hd:3 * hd] = wv_ref[:, cols].astype(BF16)
        w_s[t, :, 3 * hd:4 * hd] = wg_ref[:, cols].astype(BF16)
        p_s[t] = jnp.dot(h_ref[...], w_s[t], preferred_element_type=F32)
    for t in range(RET_HEADS_PER_STEP):
        cols = slice(t * hd, (t + 1) * hd)
        _ret_head(lgf_ref[t, 0:1, 0:1], lgb_ref[t, 0:1, 0:1], p_s.at[t], kc_ref[:, cols], vc_ref[:, cols],
                  cos_ref[...], sin_ref[...], o_ref, cols)


def _ret_head(lgf, lgb, p_s, kc, vcx, cos2, sin2, o_ref, out_cols):
    hd = RET_HEAD_DIM
    l = p_s.shape[0]
    lc = kc.shape[0]
    c = RET_CHUNK
    n_chunks = l // c

    q = p_s[:, 0 * hd:1 * hd]
    p_s[:, 0 * hd:1 * hd] = q * cos2 + pltpu.roll(q, hd // 2, axis=1) * sin2
    k = p_s[:, 1 * hd:2 * hd] * K_SCALE
    p_s[:, 1 * hd:2 * hd] = k * cos2 + pltpu.roll(k, hd // 2, axis=1) * sin2

    pos_c = lax.broadcasted_iota(jnp.int32, (1, lc), 1).astype(F32)
    kct = (kc * K_SCALE).T
    s_fwd = _bdot(kct * jnp.exp(lgf * (lc - 1.0 - pos_c)), vcx)
    s_bwd = _bdot(kct * jnp.exp(lgb * pos_c), vcx)

    pos = lax.broadcasted_iota(jnp.int32, (1, c), 1).astype(F32)
    wf_row = jnp.exp(lgf * (c - 1.0 - pos))
    wb_row = jnp.exp(lgb * pos)
    kv_f, kv_b = [], []
    for n in range(n_chunks):
        rows = slice(n * c, (n + 1) * c)
        kt = p_s[rows, 1 * hd:2 * hd].T
        kw = jnp.concatenate([kt * wf_row, kt * wb_row], axis=0)
        kv = _bdot(kw, p_s[rows, 2 * hd:3 * hd])
        kv_f.append(kv[:hd])
        kv_b.append(kv[hd:])

    gfc = jnp.exp(lgf * float(c))
    gbc = jnp.exp(lgb * float(c))
    st_f = [s_fwd]
    for n in range(n_chunks - 1):
        st_f.append(st_f[-1] * gfc + kv_f[n])
    st_b = [s_bwd]
    for n in range(n_chunks - 1, 0, -1):
        st_b.append(st_b[-1] * gbc + kv_b[n])
    st_b = st_b[::-1]

    ri = lax.broadcasted_iota(jnp.int32, (c, c), 0)
    ci = lax.broadcasted_iota(jnp.int32, (c, c), 1)
    diff = (ri - ci).astype(F32)
    decay = jnp.where(diff > 0, jnp.exp(lgf * jnp.maximum(diff, 0.0)),
                      jnp.where(diff < 0, jnp.exp(lgb * jnp.maximum(-diff, 0.0)), 2.0))
    rpos = lax.broadcasted_iota(jnp.int32, (c, 1), 0).astype(F32)
    qf_col = jnp.exp(lgf * (rpos + 1.0))
    qb_col = jnp.exp(lgb * (c - rpos))

    for n in range(n_chunks):
        rows = slice(n * c, (n + 1) * c)
        qn = p_s[rows, 0 * hd:1 * hd]
        kn = p_s[rows, 1 * hd:2 * hd]
        vn = p_s[rows, 2 * hd:3 * hd]
        gn = p_s[rows, 3 * hd:4 * hd]
        scores = lax.dot_general(qn.astype(BF16), kn.astype(BF16), (((1,), (1,)), ((), ())),
                                 preferred_element_type=F32)
        o = _bdot(scores * decay, vn)
        qd = jnp.concatenate([qn * qf_col, qn * qb_col], axis=1)
        st = jnp.concatenate([st_f[n], st_b[n]], axis=0)
        o = o + _bdot(qd, st)
        mu = jnp.mean(o, axis=-1, keepdims=True)
        oc = o - mu
        var = jnp.mean(oc * oc, axis=-1, keepdims=True)
        o_ref[rows, out_cols] = (oc * lax.rsqrt(var + GN_EPS) * _silu(gn)).astype(BF16)


def _retention(h, w_in, e, kv_ctx, lg_f, lg_b, cos2, sin2):
    nb, l, d = h.shape
    lc = kv_ctx.shape[1]
    hd = RET_HEAD_DIM
    g = RET_HEADS_PER_STEP
    ng = RET_HEADS // g
    lgf = jnp.broadcast_to(lg_f.astype(F32)[:, None, None], (RET_HEADS, 8, 128))
    lgb = jnp.broadcast_to(lg_b.astype(F32)[:, None, None], (RET_HEADS, 8, 128))

    def wspec(part):
        return pl.BlockSpec((None, d, g * hd), lambda b, n: (e, 0, part * ng + n))

    return pl.pallas_call(
        _ret_kernel,
        grid=(nb, ng),
        in_specs=[
            pl.BlockSpec((g, 8, 128), lambda b, n: (n, 0, 0)),
            pl.BlockSpec((g, 8, 128), lambda b, n: (n, 0, 0)),
            pl.BlockSpec((None, l, d), lambda b, n: (b, 0, 0)),
            wspec(0), wspec(1), wspec(2), wspec(3),
            pl.BlockSpec((None, lc, g * hd), lambda b, n: (b, 0, n)),
            pl.BlockSpec((None, lc, g * hd), lambda b, n: (b, 0, ng + n)),
            pl.BlockSpec((l, hd), lambda b, n: (0, 0)),
            pl.BlockSpec((l, hd), lambda b, n: (0, 0)),
        ],
        out_specs=pl.BlockSpec((None, l, g * hd), lambda b, n: (b, 0, n)),
        out_shape=jax.ShapeDtypeStruct((nb, l, RET_W), BF16),
        scratch_shapes=[pltpu.VMEM((g, d, 4 * hd), BF16), pltpu.VMEM((g, l, 4 * hd), F32)],
        compiler_params=_params(2),
        name="retention",
    )(lgf, lgb, h, w_in, w_in, w_in, w_in, kv_ctx, kv_ctx, cos2, sin2)


CONV_COL_TILE = 256


def _conv_kernel(h_ref, wb_ref, wc_ref, wu_ref, wconv_ref, o_ref, w_s, p_s):
    tc = CONV_COL_TILE
    l = h_ref.shape[0]
    w_s[:, 0 * tc:1 * tc] = wb_ref[...].astype(BF16)
    w_s[:, 1 * tc:2 * tc] = wc_ref[...].astype(BF16)
    w_s[:, 2 * tc:3 * tc] = wu_ref[...].astype(BF16)
    p_s[...] = jnp.dot(h_ref[...], w_s[...], preferred_element_type=F32)
    z = p_s[:, 1 * tc:2 * tc] * p_s[:, 2 * tc:3 * tc]
    t = lax.broadcasted_iota(jnp.int32, (l, 1), 0)
    z_prev = jnp.where(t == 0, 0.0, pltpu.roll(z, 1, axis=0))
    z_next = jnp.where(t == l - 1, 0.0, pltpu.roll(z, l - 1, axis=0))
    conv = z_prev * wconv_ref[0:1, :] + z * wconv_ref[1:2, :] + z_next * wconv_ref[2:3, :]
    o_ref[...] = (p_s[:, 0 * tc:1 * tc] * conv).astype(BF16)


def _gated_conv(h, w_in, w_conv, e):
    nb, l, d = h.shape
    tc = CONV_COL_TILE
    nt = CONV_W // tc
    col0 = 4 * RET_W // tc

    def wspec(part):
        return pl.BlockSpec((None, d, tc), lambda b, n: (e, 0, col0 + part * nt + n))

    return pl.pallas_call(
        _conv_kernel,
        grid=(nb, nt),
        in_specs=[
            pl.BlockSpec((None, l, d), lambda b, n: (b, 0, 0)),
            wspec(0), wspec(1), wspec(2),
            pl.BlockSpec((None, 3, tc), lambda b, n: (e, 0, n)),
        ],
        out_specs=pl.BlockSpec((None, l, tc), lambda b, n: (b, 0, n)),
        out_shape=jax.ShapeDtypeStruct((nb, l, CONV_W), BF16),
        scratch_shapes=[pltpu.VMEM((d, 3 * tc), BF16), pltpu.VMEM((l, 3 * tc), F32)],
        compiler_params=_params(2),
        name="gated_conv",
    )(h, w_in, w_in, w_in, w_conv)


def _outproj_kernel(ar_ref, ac_ref, wr_ref, wc_ref, x_ref, mod_ref, o_ref):
    y = jnp.dot(ar_ref[...], wr_ref[...].astype(BF16), preferred_element_type=F32)
    y = y + jnp.dot(ac_ref[...], wc_ref[...].astype(BF16), preferred_element_type=F32)
    o_ref[...] = x_ref[...] + mod_ref[5:6, :] * y


def _outproj(a_ret, a_conv, w_out, e, x, mod):
    nb, l, d = x.shape
    tn = OUT_COL_TILE
    return pl.pallas_call(
        _outproj_kernel,
        grid=(nb, d // tn),
        in_specs=[
            pl.BlockSpec((None, l, RET_W), lambda b, n: (b, 0, 0)),
            pl.BlockSpec((None, l, CONV_W), lambda b, n: (b, 0, 0)),
            pl.BlockSpec((None, RET_W, tn), lambda b, n: (e, 0, n)),
            pl.BlockSpec((None, CONV_W, tn), lambda b, n: (e, 1, n)),
            pl.BlockSpec((None, l, tn), lambda b, n: (b, 0, n)),
            pl.BlockSpec((None, N_MOD, tn), lambda b, n: (b, 0, n)),
        ],
        out_specs=pl.BlockSpec((None, l, tn), lambda b, n: (b, 0, n)),
        out_shape=jax.ShapeDtypeStruct((nb, l, d), F32),
        compiler_params=_params(2),
        name="outproj",
    )(a_ret, a_conv, w_out, w_out, x, mod)


POOL_ROW_CHUNK = 256
POOL_HALO = 16


def _pool_kernel(h_ref, w_ref, ps_ref, x_ref, mod_ref, o_ref, hp_s, p_s):
    l, gw = h_ref.shape
    r, halo = POOL_ROW_CHUNK, POOL_HALO
    g = pl.program_id(1)
    lo = sum(jnp.where(g == gi, w // 2, 0) for gi, w in enumerate(POOL_WINDOWS))
    hi = sum(jnp.where(g == gi, w - w // 2, 0) for gi, w in enumerate(POOL_WINDOWS))

    hp_s[0:halo, :] = jnp.zeros((halo, gw), BF16)
    hp_s[halo + l:, :] = jnp.zeros((halo, gw), BF16)
    hp_s[halo:halo + l, :] = h_ref[...]

    ri = lax.broadcasted_iota(jnp.int32, (r, r + 2 * halo), 0)
    ci = lax.broadcasted_iota(jnp.int32, (r, r + 2 * halo), 1)
    off = ci - halo - ri
    band = jnp.where(off >= -lo, jnp.where(off < hi, 1.0, 0.0), 0.0).astype(BF16)
    for n in range(l // r):
        t = n * r + lax.broadcasted_iota(jnp.int32, (r, 1), 0)
        cnt = (jnp.minimum(t + hi, l) - jnp.maximum(t - lo, 0)).astype(F32)
        win = jnp.dot(band, hp_s[n * r:(n + 1) * r + 2 * halo, :], preferred_element_type=F32)
        hc = hp_s[n * r + halo:(n + 1) * r + halo, :].astype(F32)
        p_s[n * r:(n + 1) * r, :] = (win / cnt - hc).astype(BF16)

    y = jnp.dot(p_s[...], w_ref[...].astype(BF16), preferred_element_type=F32) * ps_ref[...]
    o_ref[...] = x_ref[...] + mod_ref[5:6, :] * y


def _pool(h, pool_w, pool_scale, o, x, mod):
    nb, l, d = x.shape
    gw = POOL_GROUP_W
    ng = d // gw
    assert max(POOL_WINDOWS) // 2 <= POOL_HALO and l % POOL_ROW_CHUNK == 0
    return pl.pallas_call(
        _pool_kernel,
        grid=(nb, ng),
        in_specs=[
            pl.BlockSpec((None, l, gw), lambda b, g: (b, 0, g)),
            pl.BlockSpec((None, None, gw, gw), lambda b, g: (o, g, 0, 0)),
            pl.BlockSpec((None, 1, gw), lambda b, g: (o, 0, g)),
            pl.BlockSpec((None, l, gw), lambda b, g: (b, 0, g)),
            pl.BlockSpec((None, N_MOD, gw), lambda b, g: (b, 0, g)),
        ],
        out_specs=pl.BlockSpec((None, l, gw), lambda b, g: (b, 0, g)),
        out_shape=jax.ShapeDtypeStruct((nb, l, d), F32),
        scratch_shapes=[pltpu.VMEM((l + 2 * POOL_HALO, gw), BF16), pltpu.VMEM((l, gw), BF16)],
        compiler_params=_params(2),
        name="pool",
    )(h, pool_w, pool_scale.reshape(-1, 1, d), x, mod)


def _rope_tables(l):
    quarter = RET_HEAD_DIM // 4
    inv = ROPE_BASE ** (-jnp.arange(quarter, dtype=F32) / quarter)
    rows = (jnp.arange(l) // GRID_W).astype(F32)
    cols = (jnp.arange(l) % GRID_W).astype(F32)
    ang = jnp.concatenate([rows[:, None] * inv, cols[:, None] * inv], axis=-1)
    cos, sin = jnp.cos(ang), jnp.sin(ang)
    return jnp.concatenate([cos, cos], axis=-1), jnp.concatenate([-sin, sin], axis=-1)


def kernel(x, c, ctx, c_ctx, w_mod, b_mod, norm_ffn1, norm_mix, norm_ffn2, ffn1_w_gate, ffn1_w_up, ffn1_w_down,
           ffn2_w_gate, ffn2_w_up, ffn2_w_down, mix_w_in, mix_w_conv, mix_w_out, ret_decay_fwd, ret_decay_bwd,
           pool_w, pool_scale, final_norm):
    b, l, d = x.shape
    depth = w_mod.shape[0]
    lc = ctx.shape[1]
    assert b + 1 <= 8 and d == D_MODEL
    last_even = ((depth - 1) // 2) * 2

    cond8 = jnp.zeros((8, d), F32).at[:b].set(c).at[b].set(c_ctx)
    mod_all = _adaln(cond8, w_mod, b_mod).reshape(depth, 8, N_MOD, d)
    cos2, sin2 = _rope_tables(l)

    xc = ctx.reshape(1, b * lc, d)
    h = None
    for li in range(depth):
        mod = mod_all[li, :b]
        mod_ctx = mod_all[li, b:b + 1]
        ctx_needed = li <= last_even
        ctx_full = li < last_even
        ffn1_w = (ffn1_w_gate, ffn1_w_up, ffn1_w_down)
        if ctx_needed:
            xc, hc, w16 = _ffn(xc, mod_ctx, norm_ffn1[li], ffn1_w, norm_mix[li], li=li, row0=0, post='mix')
            x, h, _ = _ffn(x, mod, norm_ffn1[li], w16, norm_mix[li], row0=0, post='mix')
        else:
            x, h = _ffn_stream(x, mod, norm_ffn1[li], ffn1_w, norm_mix[li], li=li, row0=0, post='mix', hn_buf=h)
        if li % 2 == 0:
            e = li // 2
            assert not ctx_full, "a context stream that is read again by a later layer is not implemented"
            lg_f = jax.nn.log_sigmoid(ret_decay_fwd[e])
            lg_b = jax.nn.log_sigmoid(ret_decay_bwd[e])
            kv_ctx = _ctx_kv(hc[0], mix_w_in, e).reshape(b, lc, 2 * RET_W)
            a_ret = _retention(h, mix_w_in, e, kv_ctx, lg_f, lg_b, cos2, sin2)
            a_conv = _gated_conv(h, mix_w_in, mix_w_conv, e)
            x = _outproj(a_ret, a_conv, mix_w_out, e, x, mod)
        else:
            x = _pool(h, pool_w, pool_scale, li // 2, x, mod)
        x, _ = _ffn_stream(x, mod, norm_ffn2[li], (ffn2_w_gate, ffn2_w_up, ffn2_w_down), final_norm,
                           li=li, row0=6, post=('final' if li == depth - 1 else None))
    return x
```

```python
import functools

import jax
import jax.numpy as jnp
from jax import lax
from jax.experimental import pallas as pl
from jax.experimental.pallas import tpu as pltpu

D_MODEL = 2048
GRID_W = 64
RET_HEADS = 8
RET_HEAD_DIM = 128
RET_W = RET_HEADS * RET_HEAD_DIM
K_SCALE = RET_HEAD_DIM ** -0.5
ROPE_BASE = 10000.0
CONV_W = D_MODEL - RET_W
POOL_WINDOWS = (2, 4, 8, 16)
POOL_GROUP_W = D_MODEL // len(POOL_WINDOWS)
N_MOD = 9
MACARON = 0.5
EPS = 1e-6
GN_EPS = 1e-5

F32 = jnp.float32
BF16 = jnp.bfloat16

V7X_VMEM_BYTES = 64 * 1024 * 1024
VMEM_LIMIT = V7X_VMEM_BYTES - 7 * 1024 * 1024

FFN_TOKEN_TILE = 1024
FFN_FF_TILE = 256
FFN_FF_TILE_BF16 = 512
FFN_ROW_CHUNK = 128
FFN_EDGE_ROW_BLOCK = 512
FFN_DOWN_CHUNK = 512
NORM_COL_BLOCK = 128
ADALN_COL_TILE = 1024
CTX_KV_COL_TILE = 256
RET_CHUNK = 256
RET_HEADS_PER_STEP = 2
OUT_ROW_TILE = 512
OUT_COL_CHUNK = 512


def _params(n_axes, vmem=VMEM_LIMIT):
    return pltpu.CompilerParams(dimension_semantics=("arbitrary",) * n_axes, vmem_limit_bytes=vmem)


def _silu(v):
    return v / (1.0 + jnp.exp(-v))


def _bdot(a, b):
    return jnp.dot(a.astype(BF16), b.astype(BF16), preferred_element_type=F32)


def _adaln_kernel(cond_ref, w_ref, b_ref, o_ref):
    s = _silu(cond_ref[...])
    o_ref[...] = _bdot(s, w_ref[...]) + b_ref[...]


def _adaln(cond8, w_mod, b_mod):
    depth, d, n = w_mod.shape
    tn = ADALN_COL_TILE
    return pl.pallas_call(
        _adaln_kernel,
        grid=(depth, n // tn),
        in_specs=[
            pl.BlockSpec((8, d), lambda l, j: (0, 0)),
            pl.BlockSpec((None, d, tn), lambda l, j: (l, 0, j)),
            pl.BlockSpec((None, 1, tn), lambda l, j: (l, 0, j)),
        ],
        out_specs=pl.BlockSpec((None, 8, tn), lambda l, j: (l, 0, j)),
        out_shape=jax.ShapeDtypeStruct((depth, 8, n), F32),
        compiler_params=_params(2),
        name="adaln",
    )(cond8, w_mod, b_mod.reshape(depth, 1, n))


def _col_blocks(d):
    return [slice(c0, c0 + NORM_COL_BLOCK) for c0 in range(0, d, NORM_COL_BLOCK)]


def _rstd_from_partial(acc, d):
    return lax.rsqrt(jnp.sum(acc, axis=-1, keepdims=True) * (1.0 / d) + EPS)


def _ffn_kernel(*refs, row0, post, emit_w, n_in):
    x_ref, mod_ref, gn_ref, wg_ref, wu_ref, wd_ref, pn_ref = refs[:7]
    refs = list(refs[n_in:])
    o_ref = refs.pop(0)
    hn_ref = refs.pop(0) if post == 'mix' else None
    wg16_ref, wu16_ref, wd16_ref = (refs.pop(0), refs.pop(0), refs.pop(0)) if emit_w else (None, None, None)
    (h_ref,) = refs if post != 'mix' else (hn_ref,)
    j = pl.program_id(1)
    nj = pl.num_programs(1)
    tm, d = x_ref.shape
    rc = FFN_ROW_CHUNK
    blocks = _col_blocks(d)
    hids = [slice(f0, f0 + FFN_FF_TILE) for f0 in range(0, wg_ref.shape[1], FFN_FF_TILE)]

    def sub_tile(hid, rows, first):
        wg = wg_ref[:, hid].astype(BF16)
        wu = wu_ref[:, hid].astype(BF16)
        if emit_w:
            wg16_ref[:, hid] = wg
            wu16_ref[:, hid] = wu
        gate = jnp.dot(h_ref[rows, :], wg, preferred_element_type=F32)
        up = jnp.dot(h_ref[rows, :], wu, preferred_element_type=F32)
        a = (_silu(gate) * up).astype(BF16)
        for n0 in range(0, d, FFN_DOWN_CHUNK):
            cols = slice(n0, n0 + FFN_DOWN_CHUNK)
            wd = wd_ref[hid, cols].astype(BF16)
            if emit_w:
                wd16_ref[hid, cols] = wd
            y = jnp.dot(a, wd, preferred_element_type=F32)
            if first:
                o_ref[rows, cols] = y
            else:
                o_ref[rows, cols] += y

    @pl.when(j == 0)
    def _():
        shift = mod_ref[row0:row0 + 1, :]
        gs = gn_ref[...] * (1.0 + mod_ref[row0 + 1:row0 + 2, :])
        for r0 in range(0, tm, FFN_EDGE_ROW_BLOCK):
            r_end = r0 + FFN_EDGE_ROW_BLOCK
            for r1 in range(r0, r_end, rc):
                rows = slice(r1, r1 + rc)
                acc = jnp.zeros((rc, NORM_COL_BLOCK), F32)
                for cols in blocks:
                    xb = x_ref[rows, cols]
                    acc = acc + xb * xb
                rstd = _rstd_from_partial(acc, d)
                for cols in blocks:
                    h_ref[rows, cols] = (x_ref[rows, cols] * rstd * gs[:, cols] + shift[:, cols]).astype(BF16)
            rows = slice(r0, r_end)
            for hid in hids:
                sub_tile(hid, rows, first=hid is hids[0])

    @pl.when(jnp.logical_and(j > 0, j < nj - 1))
    def _():
        for hid in hids:
            sub_tile(hid, slice(None), first=False)

    @pl.when(j == nj - 1)
    def _():
        res_gate = MACARON * mod_ref[row0 + 2:row0 + 3, :]
        pn = pn_ref[...]
        if post == 'mix':
            shift2 = mod_ref[3:4, :]
            gs2 = pn * (1.0 + mod_ref[4:5, :])
        for r0 in range(0, tm, FFN_EDGE_ROW_BLOCK):
            for hid in hids:
                sub_tile(hid, slice(r0, r0 + FFN_EDGE_ROW_BLOCK), first=False)
            for r1 in range(r0, r0 + FFN_EDGE_ROW_BLOCK, rc):
                rows = slice(r1, r1 + rc)
                acc = jnp.zeros((rc, NORM_COL_BLOCK), F32)
                for cols in blocks:
                    xn = x_ref[rows, cols] + res_gate[:, cols] * o_ref[rows, cols]
                    o_ref[rows, cols] = xn
                    if post is not None:
                        acc = acc + xn * xn
                if post is not None:
                    rstd = _rstd_from_partial(acc, d)
                    for cols in blocks:
                        if post == 'final':
                            o_ref[rows, cols] = o_ref[rows, cols] * rstd * pn[:, cols]
                        else:
                            hn_ref[rows, cols] = (o_ref[rows, cols] * rstd * gs2[:, cols]
                                                  + shift2[:, cols]).astype(BF16)


def _ffn(x, mod, g_norm, weights, post_norm, *, row0, post, tiles=None, li=None, hn_buf=None):
    nb, t, d = x.shape
    tm = min(FFN_TOKEN_TILE, t)
    tpb = t // tm
    emit_w = li is not None
    inplace = tiles is not None
    t0, nt = tiles if inplace else (0, nb * tpb)
    w_gate, w_up, w_down = weights
    d_ff = w_gate.shape[-1]
    tf = FFN_FF_TILE if emit_w else FFN_FF_TILE_BF16

    def rows(n, j):
        return ((t0 + n) // tpb, (t0 + n) % tpb, 0)

    if emit_w:
        w_specs = [pl.BlockSpec((None, d, tf), lambda n, j: (li, 0, j)),
                   pl.BlockSpec((None, d, tf), lambda n, j: (li, 0, j)),
                   pl.BlockSpec((None, tf, d), lambda n, j: (li, j, 0))]
    else:
        w_specs = [pl.BlockSpec((d, tf), lambda n, j: (0, j)),
                   pl.BlockSpec((d, tf), lambda n, j: (0, j)),
                   pl.BlockSpec((tf, d), lambda n, j: (j, 0))]
    in_specs = [pl.BlockSpec((None, tm, d), rows),
                pl.BlockSpec((None, N_MOD, d), lambda n, j: ((t0 + n) // tpb, 0, 0)),
                pl.BlockSpec((1, d), lambda n, j: (0, 0)),
                *w_specs,
                pl.BlockSpec((1, d), lambda n, j: (0, 0))]
    args = [x, mod, g_norm.reshape(1, d), w_gate, w_up, w_down, post_norm.reshape(1, d)]
    out_specs = [pl.BlockSpec((None, tm, d), rows)]
    out_shape = [jax.ShapeDtypeStruct((nb, t, d), F32)]
    aliases = {0: 0} if inplace else {}
    if post == 'mix':
        out_specs.append(pl.BlockSpec((None, tm, d), rows))
        out_shape.append(jax.ShapeDtypeStruct((nb, t, d), BF16))
        if inplace:
            in_specs.append(pl.BlockSpec(memory_space=pl.ANY))
            args.append(hn_buf)
            aliases[len(args) - 1] = 1
    if emit_w:
        out_specs += [pl.BlockSpec((d, tf), lambda n, j: (0, j)),
                      pl.BlockSpec((d, tf), lambda n, j: (0, j)),
                      pl.BlockSpec((tf, d), lambda n, j: (j, 0))]
        out_shape += [jax.ShapeDtypeStruct((d, d_ff), BF16), jax.ShapeDtypeStruct((d, d_ff), BF16),
                      jax.ShapeDtypeStruct((d_ff, d), BF16)]
        assert nt == 1, "the bf16 weight casts are emitted once, by a single-tile call"
    out = pl.pallas_call(
        functools.partial(_ffn_kernel, row0=row0, post=post, emit_w=emit_w, n_in=len(args)),
        grid=(nt, d_ff // tf),
        in_specs=in_specs,
        out_specs=out_specs,
        out_shape=out_shape,
        input_output_aliases=aliases,
        scratch_shapes=[] if post == 'mix' else [pltpu.VMEM((tm, d), BF16)],
        compiler_params=_params(2),
        name="ffn",
    )(*args)
    hn = out[1] if post == 'mix' else None
    w16 = tuple(out[-3:]) if emit_w else None
    return out[0], hn, w16


def _ffn_stream(x, mod, g_norm, weights, post_norm, *, li, row0, post, hn_buf=None):
    nb, t, d = x.shape
    n_tiles = nb * (t // min(FFN_TOKEN_TILE, t))
    if post == 'mix' and hn_buf is None:
        hn_buf = jnp.zeros((nb, t, d), BF16)
    x, hn, w16 = _ffn(x, mod, g_norm, weights, post_norm, row0=row0, post=post, tiles=(0, 1), li=li, hn_buf=hn_buf)
    if n_tiles > 1:
        x, hn, _ = _ffn(x, mod, g_norm, w16, post_norm, row0=row0, post=post, tiles=(1, n_tiles - 1), hn_buf=hn)
    return x, hn


def _ctx_kv_kernel(h_ref, w_ref, o_ref):
    o_ref[...] = jnp.dot(h_ref[...], w_ref[...].astype(BF16), preferred_element_type=F32)


def _ctx_kv(hc, w_in, e):
    t, d = hc.shape
    tn = CTX_KV_COL_TILE
    col0 = RET_W // tn
    return pl.pallas_call(
        _ctx_kv_kernel,
        grid=(2 * RET_W // tn,),
        in_specs=[
            pl.BlockSpec((t, d), lambda j: (0, 0)),
            pl.BlockSpec((None, d, tn), lambda j: (e, 0, col0 + j)),
        ],
        out_specs=pl.BlockSpec((t, tn), lambda j: (0, j)),
        out_shape=jax.ShapeDtypeStruct((t, 2 * RET_W), F32),
        compiler_params=_params(1),
        name="ctx_kv",
    )(hc, w_in)


def _ret_kernel(lgf_ref, lgb_ref, h_ref, wq_ref, wk_ref, wv_ref, wg_ref, kc_ref, vc_ref,
                cos_ref, sin_ref, o_ref, w_s, p_s):
    hd = RET_HEAD_DIM
    for t in range(RET_HEADS_PER_STEP):
        cols = slice(t * hd, (t + 1) * hd)
        w_s[t, :, 0 * hd:1 * hd] = wq_ref[:, cols].astype(BF16)
        w_s[t, :, 1 * hd:2 * hd] = wk_ref[:, cols].astype(BF16)
        w_s[t, :, 2 * hd:3 * hd] = wv_ref[:, cols].astype(BF16)
        w_s[t, :, 3 * hd:4 * hd] = wg_ref[:, cols].astype(BF16)
        p_s[t] = jnp.dot(h_ref[...], w_s[t], preferred_element_type=F32)
    for t in range(RET_HEADS_PER_STEP):
        cols = slice(t * hd, (t + 1) * hd)
        _ret_head(lgf_ref[t, 0:1, 0:1], lgb_ref[t, 0:1, 0:1], p_s.at[t], kc_ref[:, cols], vc_ref[:, cols],
                  cos_ref[...], sin_ref[...], o_ref, cols)


def _ret_head(lgf, lgb, p_s, kc, vcx, cos2, sin2, o_ref, out_cols):
    hd = RET_HEAD_DIM
    l = p_s.shape[0]
    lc = kc.shape[0]
    c = RET_CHUNK
    n_chunks = l // c

    q = p_s[:, 0 * hd:1 * hd]
    p_s[:, 0 * hd:1 * hd] = q * cos2 + pltpu.roll(q, hd // 2, axis=1) * sin2
    k = p_s[:, 1 * hd:2 * hd] * K_SCALE
    p_s[:, 1 * hd:2 * hd] = k * cos2 + pltpu.roll(k, hd // 2, axis=1) * sin2

    pos_c = lax.broadcasted_iota(jnp.int32, (1, lc), 1).astype(F32)
    kct = (kc * K_SCALE).T
    s_fwd = _bdot(kct * jnp.exp(lgf * (lc - 1.0 - pos_c)), vcx)
    s_bwd = _bdot(kct * jnp.exp(lgb * pos_c), vcx)

    pos = lax.broadcasted_iota(jnp.int32, (1, c), 1).astype(F32)
    wf_row = jnp.exp(lgf * (c - 1.0 - pos))
    wb_row = jnp.exp(lgb * pos)
    kv_f, kv_b = [], []
    for n in range(n_chunks):
        rows = slice(n * c, (n + 1) * c)
        kt = p_s[rows, 1 * hd:2 * hd].T
        kw = jnp.concatenate([kt * wf_row, kt * wb_row], axis=0)
        kv = _bdot(kw, p_s[rows, 2 * hd:3 * hd])
        kv_f.append(kv[:hd])
        kv_b.append(kv[hd:])

    gfc = jnp.exp(lgf * float(c))
    gbc = jnp.exp(lgb * float(c))
    st_f = [s_fwd]
    for n in range(n_chunks - 1):
        st_f.append(st_f[-1] * gfc + kv_f[n])
    st_b = [s_bwd]
    for n in range(n_chunks - 1, 0, -1):
        st_b.append(st_b[-1] * gbc + kv_b[n])
    st_b = st_b[::-1]

    ri = lax.broadcasted_iota(jnp.int32, (c, c), 0)
    ci = lax.broadcasted_iota(jnp.int32, (c, c), 1)
    diff = (ri - ci).astype(F32)
    decay = jnp.where(diff > 0, jnp.exp(lgf * jnp.maximum(diff, 0.0)),
                      jnp.where(diff < 0, jnp.exp(lgb * jnp.maximum(-diff, 0.0)), 2.0))
    rpos = lax.broadcasted_iota(jnp.int32, (c, 1), 0).astype(F32)
    qf_col = jnp.exp(lgf * (rpos + 1.0))
    qb_col = jnp.exp(lgb * (c - rpos))

    for n in range(n_chunks):
        rows = slice(n * c, (n + 1) * c)
        qn = p_s[rows, 0 * hd:1 * hd]
        kn = p_s[rows, 1 * hd:2 * hd]
        vn = p_s[rows, 2 * hd:3 * hd]
        gn = p_s[rows, 3 * hd:4 * hd]
        scores = lax.dot_general(qn.astype(BF16), kn.astype(BF16), (((1,), (1,)), ((), ())),
                                 preferred_element_type=F32)
        o = _bdot(scores * decay, vn)
        qd = jnp.concatenate([qn * qf_col, qn * qb_col], axis=1)
        st = jnp.concatenate([st_f[n], st_b[n]], axis=0)
        o = o + _bdot(qd, st)
        mu = jnp.mean(o, axis=-1, keepdims=True)
        oc = o - mu
        var = jnp.mean(oc * oc, axis=-1, keepdims=True)
        o_ref[rows, out_cols] = (oc * lax.rsqrt(var + GN_EPS) * _silu(gn)).astype(BF16)


def _retention(h, w_in, e, kv_ctx, lg_f, lg_b, cos2, sin2):
    nb, l, d = h.shape
    lc = kv_ctx.shape[1]
    hd = RET_HEAD_DIM
    g = RET_HEADS_PER_STEP
    ng = RET_HEADS // g
    lgf = jnp.broadcast_to(lg_f.astype(F32)[:, None, None], (RET_HEADS, 8, 128))
    lgb = jnp.broadcast_to(lg_b.astype(F32)[:, None, None], (RET_HEADS, 8, 128))

    def wspec(part):
        return pl.BlockSpec((None, d, g * hd), lambda b, n: (e, 0, part * ng + n))

    return pl.pallas_call(
        _ret_kernel,
        grid=(nb, ng),
        in_specs=[
            pl.BlockSpec((g, 8, 128), lambda b, n: (n, 0, 0)),
            pl.BlockSpec((g, 8, 128), lambda b, n: (n, 0, 0)),
            pl.BlockSpec((None, l, d), lambda b, n: (b, 0, 0)),
            wspec(0), wspec(1), wspec(2), wspec(3),
            pl.BlockSpec((None, lc, g * hd), lambda b, n: (b, 0, n)),
            pl.BlockSpec((None, lc, g * hd), lambda b, n: (b, 0, ng + n)),
            pl.BlockSpec((l, hd), lambda b, n: (0, 0)),
            pl.BlockSpec((l, hd), lambda b, n: (0, 0)),
        ],
        out_specs=pl.BlockSpec((None, l, g * hd), lambda b, n: (b, 0, n)),
        out_shape=jax.ShapeDtypeStruct((nb, l, RET_W), BF16),
        scratch_shapes=[pltpu.VMEM((g, d, 4 * hd), BF16), pltpu.VMEM((g, l, 4 * hd), F32)],
        compiler_params=_params(2),
        name="retention",
    )(lgf, lgb, h, w_in, w_in, w_in, w_in, kv_ctx, kv_ctx, cos2, sin2)


CONV_COL_TILE = 256


def _conv_kernel(h_ref, wb_ref, wc_ref, wu_ref, wconv_ref, wo_ref, o_ref, wo16_ref, w_s, p_s):
    tc = CONV_COL_TILE
    wo16_ref[...] = wo_ref[...].astype(BF16)
    l = h_ref.shape[0]
    w_s[:, 0 * tc:1 * tc] = wb_ref[...].astype(BF16)
    w_s[:, 1 * tc:2 * tc] = wc_ref[...].astype(BF16)
    w_s[:, 2 * tc:3 * tc] = wu_ref[...].astype(BF16)
    p_s[...] = jnp.dot(h_ref[...], w_s[...], preferred_element_type=F32)
    z = p_s[:, 1 * tc:2 * tc] * p_s[:, 2 * tc:3 * tc]
    t = lax.broadcasted_iota(jnp.int32, (l, 1), 0)
    z_prev = jnp.where(t == 0, 0.0, pltpu.roll(z, 1, axis=0))
    z_next = jnp.where(t == l - 1, 0.0, pltpu.roll(z, l - 1, axis=0))
    conv = z_prev * wconv_ref[0:1, :] + z * wconv_ref[1:2, :] + z_next * wconv_ref[2:3, :]
    o_ref[...] = (p_s[:, 0 * tc:1 * tc] * conv).astype(BF16)


def _gated_conv(h, w_in, w_conv, w_out, e):
    nb, l, d = h.shape
    tc = CONV_COL_TILE
    nt = CONV_W // tc
    col0 = 4 * RET_W // tc
    slab = d // (nb * nt)
    assert slab * nb * nt == d and slab % 16 == 0

    def wspec(part):
        return pl.BlockSpec((None, d, tc), lambda b, n: (e, 0, col0 + part * nt + n))

    return pl.pallas_call(
        _conv_kernel,
        grid=(nb, nt),
        in_specs=[
            pl.BlockSpec((None, l, d), lambda b, n: (b, 0, 0)),
            wspec(0), wspec(1), wspec(2),
            pl.BlockSpec((None, 3, tc), lambda b, n: (e, 0, n)),
            pl.BlockSpec((None, slab, d), lambda b, n: (e, b * nt + n, 0)),
        ],
        out_specs=[pl.BlockSpec((None, l, tc), lambda b, n: (b, 0, n)),
                   pl.BlockSpec((slab, d), lambda b, n: (b * nt + n, 0))],
        out_shape=[jax.ShapeDtypeStruct((nb, l, CONV_W), BF16), jax.ShapeDtypeStruct((d, d), BF16)],
        scratch_shapes=[pltpu.VMEM((d, 3 * tc), BF16), pltpu.VMEM((l, 3 * tc), F32)],
        compiler_params=_params(2),
        name="gated_conv",
    )(h, w_in, w_in, w_in, w_conv, w_out)


def _outproj_kernel(ar_ref, ac_ref, w_ref, x_ref, mod_ref, o_ref):
    d = o_ref.shape[1]
    for n0 in range(0, d, OUT_COL_CHUNK):
        cols = slice(n0, n0 + OUT_COL_CHUNK)
        y = jnp.dot(ar_ref[...], w_ref[:RET_W, cols], preferred_element_type=F32)
        y = y + jnp.dot(ac_ref[...], w_ref[RET_W:, cols], preferred_element_type=F32)
        o_ref[:, cols] = x_ref[:, cols] + mod_ref[5:6, cols] * y


def _outproj(a_ret, a_conv, w_out16, x, mod):
    nb, l, d = x.shape
    tm = OUT_ROW_TILE
    return pl.pallas_call(
        _outproj_kernel,
        grid=(nb, l // tm),
        in_specs=[
            pl.BlockSpec((None, tm, RET_W), lambda b, i: (b, i, 0)),
            pl.BlockSpec((None, tm, CONV_W), lambda b, i: (b, i, 0)),
            pl.BlockSpec((d, d), lambda b, i: (0, 0)),
            pl.BlockSpec((None, tm, d), lambda b, i: (b, i, 0)),
            pl.BlockSpec((None, N_MOD, d), lambda b, i: (b, 0, 0)),
        ],
        out_specs=pl.BlockSpec((None, tm, d), lambda b, i: (b, i, 0)),
        out_shape=jax.ShapeDtypeStruct((nb, l, d), F32),
        compiler_params=_params(2),
        name="outproj",
    )(a_ret, a_conv, w_out16, x, mod)


POOL_ROW_CHUNK = 256
POOL_HALO = 16


def _pool_kernel(h_ref, w_ref, ps_ref, x_ref, mod_ref, o_ref, hp_s, p_s):
    l, gw = h_ref.shape
    r, halo = POOL_ROW_CHUNK, POOL_HALO
    g = pl.program_id(1)
    lo = sum(jnp.where(g == gi, w // 2, 0) for gi, w in enumerate(POOL_WINDOWS))
    hi = sum(jnp.where(g == gi, w - w // 2, 0) for gi, w in enumerate(POOL_WINDOWS))

    hp_s[0:halo, :] = jnp.zeros((halo, gw), BF16)
    hp_s[halo + l:, :] = jnp.zeros((halo, gw), BF16)
    hp_s[halo:halo + l, :] = h_ref[...]

    ri = lax.broadcasted_iota(jnp.int32, (r, r + 2 * halo), 0)
    ci = lax.broadcasted_iota(jnp.int32, (r, r + 2 * halo), 1)
    off = ci - halo - ri
    band = jnp.where(off >= -lo, jnp.where(off < hi, 1.0, 0.0), 0.0).astype(BF16)
    for n in range(l // r):
        t = n * r + lax.broadcasted_iota(jnp.int32, (r, 1), 0)
        cnt = (jnp.minimum(t + hi, l) - jnp.maximum(t - lo, 0)).astype(F32)
        win = jnp.dot(band, hp_s[n * r:(n + 1) * r + 2 * halo, :], preferred_element_type=F32)
        hc = hp_s[n * r + halo:(n + 1) * r + halo, :].astype(F32)
        p_s[n * r:(n + 1) * r, :] = (win / cnt - hc).astype(BF16)

    y = jnp.dot(p_s[...], w_ref[...].astype(BF16), preferred_element_type=F32) * ps_ref[...]
    o_ref[...] = x_ref[...] + mod_ref[5:6, :] * y


def _pool(h, pool_w, pool_scale, o, x, mod):
    nb, l, d = x.shape
    gw = POOL_GROUP_W
    ng = d // gw
    assert max(POOL_WINDOWS) // 2 <= POOL_HALO and l % POOL_ROW_CHUNK == 0
    return pl.pallas_call(
        _pool_kernel,
        grid=(nb, ng),
        in_specs=[
            pl.BlockSpec((None, l, gw), lambda b, g: (b, 0, g)),
            pl.BlockSpec((None, None, gw, gw), lambda b, g: (o, g, 0, 0)),
            pl.BlockSpec((None, 1, gw), lambda b, g: (o, 0, g)),
            pl.BlockSpec((None, l, gw), lambda b, g: (b, 0, g)),
            pl.BlockSpec((None, N_MOD, gw), lambda b, g: (b, 0, g)),
        ],
        out_specs=pl.BlockSpec((None, l, gw), lambda b, g: (b, 0, g)),
        out_shape=jax.ShapeDtypeStruct((nb, l, d), F32),
        scratch_shapes=[pltpu.VMEM((l + 2 * POOL_HALO, gw), BF16), pltpu.VMEM((l, gw), BF16)],
        compiler_params=_params(2),
        name="pool",
    )(h, pool_w, pool_scale.reshape(-1, 1, d), x, mod)


def _rope_tables(l):
    quarter = RET_HEAD_DIM // 4
    inv = ROPE_BASE ** (-jnp.arange(quarter, dtype=F32) / quarter)
    rows = (jnp.arange(l) // GRID_W).astype(F32)
    cols = (jnp.arange(l) % GRID_W).astype(F32)
    ang = jnp.concatenate([rows[:, None] * inv, cols[:, None] * inv], axis=-1)
    cos, sin = jnp.cos(ang), jnp.sin(ang)
    return jnp.concatenate([cos, cos], axis=-1), jnp.concatenate([-sin, sin], axis=-1)


def kernel(x, c, ctx, c_ctx, w_mod, b_mod, norm_ffn1, norm_mix, norm_ffn2, ffn1_w_gate, ffn1_w_up, ffn1_w_down,
           ffn2_w_gate, ffn2_w_up, ffn2_w_down, mix_w_in, mix_w_conv, mix_w_out, ret_decay_fwd, ret_decay_bwd,
           pool_w, pool_scale, final_norm):
    b, l, d = x.shape
    depth = w_mod.shape[0]
    lc = ctx.shape[1]
    assert b + 1 <= 8 and d == D_MODEL
    last_even = ((depth - 1) // 2) * 2

    cond8 = jnp.zeros((8, d), F32).at[:b].set(c).at[b].set(c_ctx)
    mod_all = _adaln(cond8, w_mod, b_mod).reshape(depth, 8, N_MOD, d)
    cos2, sin2 = _rope_tables(l)

    xc = ctx.reshape(1, b * lc, d)
    h = None
    for li in range(depth):
        mod = mod_all[li, :b]
        mod_ctx = mod_all[li, b:b + 1]
        ctx_needed = li <= last_even
        ctx_full = li < last_even
        ffn1_w = (ffn1_w_gate, ffn1_w_up, ffn1_w_down)
        if ctx_needed:
            xc, hc, w16 = _ffn(xc, mod_ctx, norm_ffn1[li], ffn1_w, norm_mix[li], li=li, row0=0, post='mix')
            x, h, _ = _ffn(x, mod, norm_ffn1[li], w16, norm_mix[li], row0=0, post='mix')
        else:
            x, h = _ffn_stream(x, mod, norm_ffn1[li], ffn1_w, norm_mix[li], li=li, row0=0, post='mix', hn_buf=h)
        if li % 2 == 0:
            e = li // 2
            assert not ctx_full, "a context stream that is read again by a later layer is not implemented"
            lg_f = jax.nn.log_sigmoid(ret_decay_fwd[e])
            lg_b = jax.nn.log_sigmoid(ret_decay_bwd[e])
            kv_ctx = _ctx_kv(hc[0], mix_w_in, e).reshape(b, lc, 2 * RET_W)
            a_ret = _retention(h, mix_w_in, e, kv_ctx, lg_f, lg_b, cos2, sin2)
            a_conv, w_out16 = _gated_conv(h, mix_w_in, mix_w_conv, mix_w_out, e)
            x = _outproj(a_ret, a_conv, w_out16, x, mod)
        else:
            x = _pool(h, pool_w, pool_scale, li // 2, x, mod)
        x, _ = _ffn_stream(x, mod, norm_ffn2[li], (ffn2_w_gate, ffn2_w_up, ffn2_w_down), final_norm,
                           li=li, row0=6, post=('final' if li == depth - 1 else None))
    return x
```

```python
import functools

import jax
import jax.numpy as jnp
from jax import lax
from jax.experimental import pallas as pl
from jax.experimental.pallas import tpu as pltpu

D_MODEL = 2048
GRID_W = 64
RET_HEADS = 8
RET_HEAD_DIM = 128
RET_W = RET_HEADS * RET_HEAD_DIM
K_SCALE = RET_HEAD_DIM ** -0.5
ROPE_BASE = 10000.0
CONV_W = D_MODEL - RET_W
POOL_WINDOWS = (2, 4, 8, 16)
POOL_GROUP_W = D_MODEL // len(POOL_WINDOWS)
N_MOD = 9
MACARON = 0.5
EPS = 1e-6
GN_EPS = 1e-5

F32 = jnp.float32
BF16 = jnp.bfloat16

V7X_VMEM_BYTES = 64 * 1024 * 1024
VMEM_LIMIT = V7X_VMEM_BYTES - 7 * 1024 * 1024

FFN_TOKEN_TILE = 1024
FFN_FF_TILE = 256
FFN_FF_TILE_BF16 = 512
FFN_ROW_CHUNK = 128
FFN_EDGE_ROW_BLOCK = 512
FFN_DOWN_CHUNK = 512
NORM_COL_BLOCK = 128
ADALN_COL_TILE = 2048
CTX_KV_COL_TILE = 512
RET_CHUNK = 256
RET_HEADS_PER_STEP = 2
OUT_ROW_TILE = 512
OUT_COL_CHUNK = 512


def _params(n_axes, vmem=VMEM_LIMIT):
    return pltpu.CompilerParams(dimension_semantics=("arbitrary",) * n_axes, vmem_limit_bytes=vmem)


def _silu(v):
    return v / (1.0 + jnp.exp(-v))


def _bdot(a, b):
    return jnp.dot(a.astype(BF16), b.astype(BF16), preferred_element_type=F32)


def _adaln_kernel(cond_ref, w_ref, b_ref, o_ref):
    s = _silu(cond_ref[...])
    o_ref[...] = _bdot(s, w_ref[...]) + b_ref[...]


def _adaln(cond8, w_mod, b_mod):
    depth, d, n = w_mod.shape
    tn = ADALN_COL_TILE
    return pl.pallas_call(
        _adaln_kernel,
        grid=(depth, n // tn),
        in_specs=[
            pl.BlockSpec((8, d), lambda l, j: (0, 0)),
            pl.BlockSpec((None, d, tn), lambda l, j: (l, 0, j)),
            pl.BlockSpec((None, 1, tn), lambda l, j: (l, 0, j)),
        ],
        out_specs=pl.BlockSpec((None, 8, tn), lambda l, j: (l, 0, j)),
        out_shape=jax.ShapeDtypeStruct((depth, 8, n), F32),
        compiler_params=_params(2),
        name="adaln",
    )(cond8, w_mod, b_mod.reshape(depth, 1, n))


def _col_blocks(d):
    return [slice(c0, c0 + NORM_COL_BLOCK) for c0 in range(0, d, NORM_COL_BLOCK)]


def _rstd_from_partial(acc, d):
    return lax.rsqrt(jnp.sum(acc, axis=-1, keepdims=True) * (1.0 / d) + EPS)


def _ffn_kernel(*refs, row0, post, emit_w, n_in):
    x_ref, mod_ref, gn_ref, wg_ref, wu_ref, wd_ref, pn_ref = refs[:7]
    refs = list(refs[n_in:])
    o_ref = refs.pop(0)
    hn_ref = refs.pop(0) if post == 'mix' else None
    wg16_ref, wu16_ref, wd16_ref = (refs.pop(0), refs.pop(0), refs.pop(0)) if emit_w else (None, None, None)
    (h_ref,) = refs if post != 'mix' else (hn_ref,)
    j = pl.program_id(1)
    nj = pl.num_programs(1)
    tm, d = x_ref.shape
    rc = FFN_ROW_CHUNK
    blocks = _col_blocks(d)
    hids = [slice(f0, f0 + FFN_FF_TILE) for f0 in range(0, wg_ref.shape[1], FFN_FF_TILE)]

    def sub_tile(hid, rows, first):
        wg = wg_ref[:, hid].astype(BF16)
        wu = wu_ref[:, hid].astype(BF16)
        if emit_w:
            wg16_ref[:, hid] = wg
            wu16_ref[:, hid] = wu
        gate = jnp.dot(h_ref[rows, :], wg, preferred_element_type=F32)
        up = jnp.dot(h_ref[rows, :], wu, preferred_element_type=F32)
        a = (_silu(gate) * up).astype(BF16)
        for n0 in range(0, d, FFN_DOWN_CHUNK):
            cols = slice(n0, n0 + FFN_DOWN_CHUNK)
            wd = wd_ref[hid, cols].astype(BF16)
            if emit_w:
                wd16_ref[hid, cols] = wd
            y = jnp.dot(a, wd, preferred_element_type=F32)
            if first:
                o_ref[rows, cols] = y
            else:
                o_ref[rows, cols] += y

    @pl.when(j == 0)
    def _():
        shift = mod_ref[row0:row0 + 1, :]
        gs = gn_ref[...] * (1.0 + mod_ref[row0 + 1:row0 + 2, :])
        for r0 in range(0, tm, FFN_EDGE_ROW_BLOCK):
            r_end = r0 + FFN_EDGE_ROW_BLOCK
            for r1 in range(r0, r_end, rc):
                rows = slice(r1, r1 + rc)
                acc = jnp.zeros((rc, NORM_COL_BLOCK), F32)
                for cols in blocks:
                    xb = x_ref[rows, cols]
                    acc = acc + xb * xb
                rstd = _rstd_from_partial(acc, d)
                for cols in blocks:
                    h_ref[rows, cols] = (x_ref[rows, cols] * rstd * gs[:, cols] + shift[:, cols]).astype(BF16)
            rows = slice(r0, r_end)
            for hid in hids:
                sub_tile(hid, rows, first=hid is hids[0])

    @pl.when(jnp.logical_and(j > 0, j < nj - 1))
    def _():
        for hid in hids:
            sub_tile(hid, slice(None), first=False)

    @pl.when(j == nj - 1)
    def _():
        res_gate = MACARON * mod_ref[row0 + 2:row0 + 3, :]
        pn = pn_ref[...]
        if post == 'mix':
            shift2 = mod_ref[3:4, :]
            gs2 = pn * (1.0 + mod_ref[4:5, :])
        for r0 in range(0, tm, FFN_EDGE_ROW_BLOCK):
            for hid in hids:
                sub_tile(hid, slice(r0, r0 + FFN_EDGE_ROW_BLOCK), first=False)
            for r1 in range(r0, r0 + FFN_EDGE_ROW_BLOCK, rc):
                rows = slice(r1, r1 + rc)
                acc = jnp.zeros((rc, NORM_COL_BLOCK), F32)
                for cols in blocks:
                    xn = x_ref[rows, cols] + res_gate[:, cols] * o_ref[rows, cols]
                    o_ref[rows, cols] = xn
                    if post is not None:
                        acc = acc + xn * xn
                if post is not None:
                    rstd = _rstd_from_partial(acc, d)
                    for cols in blocks:
                        if post == 'final':
                            o_ref[rows, cols] = o_ref[rows, cols] * rstd * pn[:, cols]
                        else:
                            hn_ref[rows, cols] = (o_ref[rows, cols] * rstd * gs2[:, cols]
                                                  + shift2[:, cols]).astype(BF16)


def _ffn(x, mod, g_norm, weights, post_norm, *, row0, post, tiles=None, li=None, hn_buf=None):
    nb, t, d = x.shape
    tm = min(FFN_TOKEN_TILE, t)
    tpb = t // tm
    emit_w = li is not None
    inplace = tiles is not None
    t0, nt = tiles if inplace else (0, nb * tpb)
    w_gate, w_up, w_down = weights
    d_ff = w_gate.shape[-1]
    tf = FFN_FF_TILE if emit_w else FFN_FF_TILE_BF16

    def rows(n, j):
        return ((t0 + n) // tpb, (t0 + n) % tpb, 0)

    if emit_w:
        w_specs = [pl.BlockSpec((None, d, tf), lambda n, j: (li, 0, j)),
                   pl.BlockSpec((None, d, tf), lambda n, j: (li, 0, j)),
                   pl.BlockSpec((None, tf, d), lambda n, j: (li, j, 0))]
    else:
        w_specs = [pl.BlockSpec((d, tf), lambda n, j: (0, j)),
                   pl.BlockSpec((d, tf), lambda n, j: (0, j)),
                   pl.BlockSpec((tf, d), lambda n, j: (j, 0))]
    in_specs = [pl.BlockSpec((None, tm, d), rows),
                pl.BlockSpec((None, N_MOD, d), lambda n, j: ((t0 + n) // tpb, 0, 0)),
                pl.BlockSpec((1, d), lambda n, j: (0, 0)),
                *w_specs,
                pl.BlockSpec((1, d), lambda n, j: (0, 0))]
    args = [x, mod, g_norm.reshape(1, d), w_gate, w_up, w_down, post_norm.reshape(1, d)]
    out_specs = [pl.BlockSpec((None, tm, d), rows)]
    out_shape = [jax.ShapeDtypeStruct((nb, t, d), F32)]
    aliases = {0: 0} if inplace else {}
    if post == 'mix':
        out_specs.append(pl.BlockSpec((None, tm, d), rows))
        out_shape.append(jax.ShapeDtypeStruct((nb, t, d), BF16))
        if inplace:
            in_specs.append(pl.BlockSpec(memory_space=pl.ANY))
            args.append(hn_buf)
            aliases[len(args) - 1] = 1
    if emit_w:
        out_specs += [pl.BlockSpec((d, tf), lambda n, j: (0, j)),
                      pl.BlockSpec((d, tf), lambda n, j: (0, j)),
                      pl.BlockSpec((tf, d), lambda n, j: (j, 0))]
        out_shape += [jax.ShapeDtypeStruct((d, d_ff), BF16), jax.ShapeDtypeStruct((d, d_ff), BF16),
                      jax.ShapeDtypeStruct((d_ff, d), BF16)]
        assert nt == 1, "the bf16 weight casts are emitted once, by a single-tile call"
    out = pl.pallas_call(
        functools.partial(_ffn_kernel, row0=row0, post=post, emit_w=emit_w, n_in=len(args)),
        grid=(nt, d_ff // tf),
        in_specs=in_specs,
        out_specs=out_specs,
        out_shape=out_shape,
        input_output_aliases=aliases,
        scratch_shapes=[] if post == 'mix' else [pltpu.VMEM((tm, d), BF16)],
        compiler_params=_params(2),
        name="ffn",
    )(*args)
    hn = out[1] if post == 'mix' else None
    w16 = tuple(out[-3:]) if emit_w else None
    return out[0], hn, w16


def _ffn_stream(x, mod, g_norm, weights, post_norm, *, li, row0, post, hn_buf=None):
    nb, t, d = x.shape
    n_tiles = nb * (t // min(FFN_TOKEN_TILE, t))
    if post == 'mix' and hn_buf is None:
        hn_buf = jnp.zeros((nb, t, d), BF16)
    x, hn, w16 = _ffn(x, mod, g_norm, weights, post_norm, row0=row0, post=post, tiles=(0, 1), li=li, hn_buf=hn_buf)
    if n_tiles > 1:
        x, hn, _ = _ffn(x, mod, g_norm, w16, post_norm, row0=row0, post=post, tiles=(1, n_tiles - 1), hn_buf=hn)
    return x, hn


def _ctx_kv_kernel(h_ref, w_ref, o_ref):
    o_ref[...] = jnp.dot(h_ref[...], w_ref[...].astype(BF16), preferred_element_type=F32)


def _ctx_kv(hc, w_in, e):
    t, d = hc.shape
    tn = CTX_KV_COL_TILE
    col0 = RET_W // tn
    return pl.pallas_call(
        _ctx_kv_kernel,
        grid=(2 * RET_W // tn,),
        in_specs=[
            pl.BlockSpec((t, d), lambda j: (0, 0)),
            pl.BlockSpec((None, d, tn), lambda j: (e, 0, col0 + j)),
        ],
        out_specs=pl.BlockSpec((t, tn), lambda j: (0, j)),
        out_shape=jax.ShapeDtypeStruct((t, 2 * RET_W), F32),
        compiler_params=_params(1),
        name="ctx_kv",
    )(hc, w_in)


def _ret_kernel(lgf_ref, lgb_ref, h_ref, wq_ref, wk_ref, wv_ref, wg_ref, kc_ref, vc_ref,
                cos_ref, sin_ref, o_ref, w_s, p_s):
    hd = RET_HEAD_DIM
    for t in range(RET_HEADS_PER_STEP):
        cols = slice(t * hd, (t + 1) * hd)
        w_s[t, :, 0 * hd:1 * hd] = wq_ref[:, cols].astype(BF16)
        w_s[t, :, 1 * hd:2 * hd] = wk_ref[:, cols].astype(BF16)
        w_s[t, :, 2 * hd:3 * hd] = wv_ref[:, cols].astype(BF16)
        w_s[t, :, 3 * hd:4 * hd] = wg_ref[:, cols].astype(BF16)
        p_s[t] = jnp.dot(h_ref[...], w_s[t], preferred_element_type=F32)
    for t in range(RET_HEADS_PER_STEP):
        cols = slice(t * hd, (t + 1) * hd)
        _ret_head(lgf_ref[t, 0:1, 0:1], lgb_ref[t, 0:1, 0:1], p_s.at[t], kc_ref[:, cols], vc_ref[:, cols],
                  cos_ref[...], sin_ref[...], o_ref, cols)


def _ret_head(lgf, lgb, p_s, kc, vcx, cos2, sin2, o_ref, out_cols):
    hd = RET_HEAD_DIM
    l = p_s.shape[0]
    lc = kc.shape[0]
    c = RET_CHUNK
    n_chunks = l // c

    q = p_s[:, 0 * hd:1 * hd]
    p_s[:, 0 * hd:1 * hd] = q * cos2 + pltpu.roll(q, hd // 2, axis=1) * sin2
    k = p_s[:, 1 * hd:2 * hd] * K_SCALE
    p_s[:, 1 * hd:2 * hd] = k * cos2 + pltpu.roll(k, hd // 2, axis=1) * sin2

    pos_c = lax.broadcasted_iota(jnp.int32, (1, lc), 1).astype(F32)
    kct = (kc * K_SCALE).T
    s_fwd = _bdot(kct * jnp.exp(lgf * (lc - 1.0 - pos_c)), vcx)
    s_bwd = _bdot(kct * jnp.exp(lgb * pos_c), vcx)

    pos = lax.broadcasted_iota(jnp.int32, (1, c), 1).astype(F32)
    wf_row = jnp.exp(lgf * (c - 1.0 - pos))
    wb_row = jnp.exp(lgb * pos)
    kv_f, kv_b = [], []
    for n in range(n_chunks):
        rows = slice(n * c, (n + 1) * c)
        kt = p_s[rows, 1 * hd:2 * hd].T
        kw = jnp.concatenate([kt * wf_row, kt * wb_row], axis=0)
        kv = _bdot(kw, p_s[rows, 2 * hd:3 * hd])
        kv_f.append(kv[:hd])
        kv_b.append(kv[hd:])

    gfc = jnp.exp(lgf * float(c))
    gbc = jnp.exp(lgb * float(c))
    st_f = [s_fwd]
    for n in range(n_chunks - 1):
        st_f.append(st_f[-1] * gfc + kv_f[n])
    st_b = [s_bwd]
    for n in range(n_chunks - 1, 0, -1):
        st_b.append(st_b[-1] * gbc + kv_b[n])
    st_b = st_b[::-1]

    ri = lax.broadcasted_iota(jnp.int32, (c, c), 0)
    ci = lax.broadcasted_iota(jnp.int32, (c, c), 1)
    diff = (ri - ci).astype(F32)
    decay = jnp.where(diff > 0, jnp.exp(lgf * jnp.maximum(diff, 0.0)),
                      jnp.where(diff < 0, jnp.exp(lgb * jnp.maximum(-diff, 0.0)), 2.0))
    rpos = lax.broadcasted_iota(jnp.int32, (c, 1), 0).astype(F32)
    qf_col = jnp.exp(lgf * (rpos + 1.0))
    qb_col = jnp.exp(lgb * (c - rpos))

    for n in range(n_chunks):
        rows = slice(n * c, (n + 1) * c)
        qn = p_s[rows, 0 * hd:1 * hd]
        kn = p_s[rows, 1 * hd:2 * hd]
        vn = p_s[rows, 2 * hd:3 * hd]
        gn = p_s[rows, 3 * hd:4 * hd]
        scores = lax.dot_general(qn.astype(BF16), kn.astype(BF16), (((1,), (1,)), ((), ())),
                                 preferred_element_type=F32)
        o = _bdot(scores * decay, vn)
        qd = jnp.concatenate([qn * qf_col, qn * qb_col], axis=1)
        st = jnp.concatenate([st_f[n], st_b[n]], axis=0)
        o = o + _bdot(qd, st)
        mu = jnp.mean(o, axis=-1, keepdims=True)
        oc = o - mu
        var = jnp.mean(oc * oc, axis=-1, keepdims=True)
        o_ref[rows, out_cols] = (oc * lax.rsqrt(var + GN_EPS) * _silu(gn)).astype(BF16)


def _retention(h, w_in, e, kv_ctx, lg_f, lg_b, cos2, sin2):
    nb, l, d = h.shape
    lc = kv_ctx.shape[1]
    hd = RET_HEAD_DIM
    g = RET_HEADS_PER_STEP
    ng = RET_HEADS // g
    lgf = jnp.broadcast_to(lg_f.astype(F32)[:, None, None], (RET_HEADS, 8, 128))
    lgb = jnp.broadcast_to(lg_b.astype(F32)[:, None, None], (RET_HEADS, 8, 128))

    def wspec(part):
        return pl.BlockSpec((None, d, g * hd), lambda b, n: (e, 0, part * ng + n))

    return pl.pallas_call(
        _ret_kernel,
        grid=(nb, ng),
        in_specs=[
            pl.BlockSpec((g, 8, 128), lambda b, n: (n, 0, 0)),
            pl.BlockSpec((g, 8, 128), lambda b, n: (n, 0, 0)),
            pl.BlockSpec((None, l, d), lambda b, n: (b, 0, 0)),
            wspec(0), wspec(1), wspec(2), wspec(3),
            pl.BlockSpec((None, lc, g * hd), lambda b, n: (b, 0, n)),
            pl.BlockSpec((None, lc, g * hd), lambda b, n: (b, 0, ng + n)),
            pl.BlockSpec((l, hd), lambda b, n: (0, 0)),
            pl.BlockSpec((l, hd), lambda b, n: (0, 0)),
        ],
        out_specs=pl.BlockSpec((None, l, g * hd), lambda b, n: (b, 0, n)),
        out_shape=jax.ShapeDtypeStruct((nb, l, RET_W), BF16),
        scratch_shapes=[pltpu.VMEM((g, d, 4 * hd), BF16), pltpu.VMEM((g, l, 4 * hd), F32)],
        compiler_params=_params(2),
        name="retention",
    )(lgf, lgb, h, w_in, w_in, w_in, w_in, kv_ctx, kv_ctx, cos2, sin2)


CONV_COL_TILE = 256


def _conv_kernel(h_ref, wb_ref, wc_ref, wu_ref, wconv_ref, wo_ref, o_ref, wo16_ref, w_s, p_s):
    tc = CONV_COL_TILE
    wo16_ref[...] = wo_ref[...].astype(BF16)
    l = h_ref.shape[0]
    w_s[:, 0 * tc:1 * tc] = wb_ref[...].astype(BF16)
    w_s[:, 1 * tc:2 * tc] = wc_ref[...].astype(BF16)
    w_s[:, 2 * tc:3 * tc] = wu_ref[...].astype(BF16)
    p_s[...] = jnp.dot(h_ref[...], w_s[...], preferred_element_type=F32)
    z = p_s[:, 1 * tc:2 * tc] * p_s[:, 2 * tc:3 * tc]
    t = lax.broadcasted_iota(jnp.int32, (l, 1), 0)
    z_prev = jnp.where(t == 0, 0.0, pltpu.roll(z, 1, axis=0))
    z_next = jnp.where(t == l - 1, 0.0, pltpu.roll(z, l - 1, axis=0))
    conv = z_prev * wconv_ref[0:1, :] + z * wconv_ref[1:2, :] + z_next * wconv_ref[2:3, :]
    o_ref[...] = (p_s[:, 0 * tc:1 * tc] * conv).astype(BF16)


def _gated_conv(h, w_in, w_conv, w_out, e):
    nb, l, d = h.shape
    tc = CONV_COL_TILE
    nt = CONV_W // tc
    col0 = 4 * RET_W // tc
    slab = d // (nb * nt)
    assert slab * nb * nt == d and slab % 16 == 0

    def wspec(part):
        return pl.BlockSpec((None, d, tc), lambda b, n: (e, 0, col0 + part * nt + n))

    return pl.pallas_call(
        _conv_kernel,
        grid=(nb, nt),
        in_specs=[
            pl.BlockSpec((None, l, d), lambda b, n: (b, 0, 0)),
            wspec(0), wspec(1), wspec(2),
            pl.BlockSpec((None, 3, tc), lambda b, n: (e, 0, n)),
            pl.BlockSpec((None, slab, d), lambda b, n: (e, b * nt + n, 0)),
        ],
        out_specs=[pl.BlockSpec((None, l, tc), lambda b, n: (b, 0, n)),
                   pl.BlockSpec((slab, d), lambda b, n: (b * nt + n, 0))],
        out_shape=[jax.ShapeDtypeStruct((nb, l, CONV_W), BF16), jax.ShapeDtypeStruct((d, d), BF16)],
        scratch_shapes=[pltpu.VMEM((d, 3 * tc), BF16), pltpu.VMEM((l, 3 * tc), F32)],
        compiler_params=_params(2),
        name="gated_conv",
    )(h, w_in, w_in, w_in, w_conv, w_out)


def _outproj_kernel(ar_ref, ac_ref, w_ref, x_ref, mod_ref, o_ref):
    d = o_ref.shape[1]
    for n0 in range(0, d, OUT_COL_CHUNK):
        cols = slice(n0, n0 + OUT_COL_CHUNK)
        y = jnp.dot(ar_ref[...], w_ref[:RET_W, cols], preferred_element_type=F32)
        y = y + jnp.dot(ac_ref[...], w_ref[RET_W:, cols], preferred_element_type=F32)
        o_ref[:, cols] = x_ref[:, cols] + mod_ref[5:6, cols] * y


def _outproj(a_ret, a_conv, w_out16, x, mod):
    nb, l, d = x.shape
    tm = OUT_ROW_TILE
    return pl.pallas_call(
        _outproj_kernel,
        grid=(nb, l // tm),
        in_specs=[
            pl.BlockSpec((None, tm, RET_W), lambda b, i: (b, i, 0)),
            pl.BlockSpec((None, tm, CONV_W), lambda b, i: (b, i, 0)),
            pl.BlockSpec((d, d), lambda b, i: (0, 0)),
            pl.BlockSpec((None, tm, d), lambda b, i: (b, i, 0)),
            pl.BlockSpec((None, N_MOD, d), lambda b, i: (b, 0, 0)),
        ],
        out_specs=pl.BlockSpec((None, tm, d), lambda b, i: (b, i, 0)),
        out_shape=jax.ShapeDtypeStruct((nb, l, d), F32),
        compiler_params=_params(2),
        name="outproj",
    )(a_ret, a_conv, w_out16, x, mod)


POOL_ROW_CHUNK = 256
POOL_HALO = 16


def _pool_kernel(h_ref, w_ref, ps_ref, x_ref, mod_ref, o_ref, hp_s, p_s):
    l, gw = h_ref.shape
    r, halo = POOL_ROW_CHUNK, POOL_HALO
    g = pl.program_id(1)
    lo = sum(jnp.where(g == gi, w // 2, 0) for gi, w in enumerate(POOL_WINDOWS))
    hi = sum(jnp.where(g == gi, w - w // 2, 0) for gi, w in enumerate(POOL_WINDOWS))

    hp_s[0:halo, :] = jnp.zeros((halo, gw), BF16)
    hp_s[halo + l:, :] = jnp.zeros((halo, gw), BF16)
    hp_s[halo:halo + l, :] = h_ref[...]

    ri = lax.broadcasted_iota(jnp.int32, (r, r + 2 * halo), 0)
    ci = lax.broadcasted_iota(jnp.int32, (r, r + 2 * halo), 1)
    off = ci - halo - ri
    band = jnp.where(off >= -lo, jnp.where(off < hi, 1.0, 0.0), 0.0).astype(BF16)
    for n in range(l // r):
        t = n * r + lax.broadcasted_iota(jnp.int32, (r, 1), 0)
        cnt = (jnp.minimum(t + hi, l) - jnp.maximum(t - lo, 0)).astype(F32)
        win = jnp.dot(band, hp_s[n * r:(n + 1) * r + 2 * halo, :], preferred_element_type=F32)
        hc = hp_s[n * r + halo:(n + 1) * r + halo, :].astype(F32)
        p_s[n * r:(n + 1) * r, :] = (win / cnt - hc).astype(BF16)

    y = jnp.dot(p_s[...], w_ref[...].astype(BF16), preferred_element_type=F32) * ps_ref[...]
    o_ref[...] = x_ref[...] + mod_ref[5:6, :] * y


def _pool(h, pool_w, pool_scale, o, x, mod):
    nb, l, d = x.shape
    gw = POOL_GROUP_W
    ng = d // gw
    assert max(POOL_WINDOWS) // 2 <= POOL_HALO and l % POOL_ROW_CHUNK == 0
    return pl.pallas_call(
        _pool_kernel,
        grid=(nb, ng),
        in_specs=[
            pl.BlockSpec((None, l, gw), lambda b, g: (b, 0, g)),
            pl.BlockSpec((None, None, gw, gw), lambda b, g: (o, g, 0, 0)),
            pl.BlockSpec((None, 1, gw), lambda b, g: (o, 0, g)),
            pl.BlockSpec((None, l, gw), lambda b, g: (b, 0, g)),
            pl.BlockSpec((None, N_MOD, gw), lambda b, g: (b, 0, g)),
        ],
        out_specs=pl.BlockSpec((None, l, gw), lambda b, g: (b, 0, g)),
        out_shape=jax.ShapeDtypeStruct((nb, l, d), F32),
        scratch_shapes=[pltpu.VMEM((l + 2 * POOL_HALO, gw), BF16), pltpu.VMEM((l, gw), BF16)],
        compiler_params=_params(2),
        name="pool",
    )(h, pool_w, pool_scale.reshape(-1, 1, d), x, mod)


def _rope_tables(l):
    quarter = RET_HEAD_DIM // 4
    inv = ROPE_BASE ** (-jnp.arange(quarter, dtype=F32) / quarter)
    rows = (jnp.arange(l) // GRID_W).astype(F32)
    cols = (jnp.arange(l) % GRID_W).astype(F32)
    ang = jnp.concatenate([rows[:, None] * inv, cols[:, None] * inv], axis=-1)
    cos, sin = jnp.cos(ang), jnp.sin(ang)
    return jnp.concatenate([cos, cos], axis=-1), jnp.concatenate([-sin, sin], axis=-1)


def kernel(x, c, ctx, c_ctx, w_mod, b_mod, norm_ffn1, norm_mix, norm_ffn2, ffn1_w_gate, ffn1_w_up, ffn1_w_down,
           ffn2_w_gate, ffn2_w_up, ffn2_w_down, mix_w_in, mix_w_conv, mix_w_out, ret_decay_fwd, ret_decay_bwd,
           pool_w, pool_scale, final_norm):
    b, l, d = x.shape
    depth = w_mod.shape[0]
    lc = ctx.shape[1]
    assert b + 1 <= 8 and d == D_MODEL
    last_even = ((depth - 1) // 2) * 2

    cond8 = jnp.zeros((8, d), F32).at[:b].set(c).at[b].set(c_ctx)
    mod_all = _adaln(cond8, w_mod, b_mod).reshape(depth, 8, N_MOD, d)
    cos2, sin2 = _rope_tables(l)

    xc = ctx.reshape(1, b * lc, d)
    h = None
    for li in range(depth):
        mod = mod_all[li, :b]
        mod_ctx = mod_all[li, b:b + 1]
        ctx_needed = li <= last_even
        ctx_full = li < last_even
        ffn1_w = (ffn1_w_gate, ffn1_w_up, ffn1_w_down)
        if ctx_needed:
            xc, hc, w16 = _ffn(xc, mod_ctx, norm_ffn1[li], ffn1_w, norm_mix[li], li=li, row0=0, post='mix')
            x, h, _ = _ffn(x, mod, norm_ffn1[li], w16, norm_mix[li], row0=0, post='mix')
        else:
            x, h = _ffn_stream(x, mod, norm_ffn1[li], ffn1_w, norm_mix[li], li=li, row0=0, post='mix', hn_buf=h)
        if li % 2 == 0:
            e = li // 2
            assert not ctx_full, "a context stream that is read again by a later layer is not implemented"
            lg_f = jax.nn.log_sigmoid(ret_decay_fwd[e])
            lg_b = jax.nn.log_sigmoid(ret_decay_bwd[e])
            kv_ctx = _ctx_kv(hc[0], mix_w_in, e).reshape(b, lc, 2 * RET_W)
            a_ret = _retention(h, mix_w_in, e, kv_ctx, lg_f, lg_b, cos2, sin2)
            a_conv, w_out16 = _gated_conv(h, mix_w_in, mix_w_conv, mix_w_out, e)
            x = _outproj(a_ret, a_conv, w_out16, x, mod)
        else:
            x = _pool(h, pool_w, pool_scale, li // 2, x, mod)
        x, _ = _ffn_stream(x, mod, norm_ffn2[li], (ffn2_w_gate, ffn2_w_up, ffn2_w_down), final_norm,
                           li=li, row0=6, post=('final' if li == depth - 1 else None))
    return x
```

```python
import functools

import jax
import jax.numpy as jnp
from jax import lax
from jax.experimental import pallas as pl
from jax.experimental.pallas import tpu as pltpu

D_MODEL = 2048
GRID_W = 64
RET_HEADS = 8
RET_HEAD_DIM = 128
RET_W = RET_HEADS * RET_HEAD_DIM
K_SCALE = RET_HEAD_DIM ** -0.5
ROPE_BASE = 10000.0
CONV_W = D_MODEL - RET_W
POOL_WINDOWS = (2, 4, 8, 16)
POOL_GROUP_W = D_MODEL // len(POOL_WINDOWS)
N_MOD = 9
MACARON = 0.5
EPS = 1e-6
GN_EPS = 1e-5

F32 = jnp.float32
BF16 = jnp.bfloat16

V7X_VMEM_BYTES = 64 * 1024 * 1024
VMEM_LIMIT = V7X_VMEM_BYTES - 7 * 1024 * 1024

FFN_TOKEN_TILE = 1024
FFN_FF_TILE = 256
FFN_FF_TILE_BF16 = 512
FFN_ROW_CHUNK = 128
FFN_EDGE_ROW_BLOCK = 512
FFN_DOWN_CHUNK = 512
NORM_COL_BLOCK = 128
ADALN_COL_TILE = 2048
CTX_KV_COL_TILE = 512
RET_CHUNK = 256
RET_HEADS_PER_STEP = 2
OUT_ROW_TILE = 512
OUT_COL_CHUNK = 512


def _params(n_axes, vmem=VMEM_LIMIT):
    return pltpu.CompilerParams(dimension_semantics=("arbitrary",) * n_axes, vmem_limit_bytes=vmem)


def _silu(v):
    return v / (1.0 + jnp.exp(-v))


def _bdot(a, b):
    return jnp.dot(a.astype(BF16), b.astype(BF16), preferred_element_type=F32)


def _adaln_kernel(cond_ref, w_ref, b_ref, o_ref):
    s = _silu(cond_ref[...])
    o_ref[...] = _bdot(s, w_ref[...]) + b_ref[...]


def _adaln(cond8, w_mod, b_mod):
    depth, d, n = w_mod.shape
    tn = ADALN_COL_TILE
    return pl.pallas_call(
        _adaln_kernel,
        grid=(depth, n // tn),
        in_specs=[
            pl.BlockSpec((8, d), lambda l, j: (0, 0)),
            pl.BlockSpec((None, d, tn), lambda l, j: (l, 0, j)),
            pl.BlockSpec((None, 1, tn), lambda l, j: (l, 0, j)),
        ],
        out_specs=pl.BlockSpec((None, 8, tn), lambda l, j: (l, 0, j)),
        out_shape=jax.ShapeDtypeStruct((depth, 8, n), F32),
        compiler_params=_params(2),
        name="adaln",
    )(cond8, w_mod, b_mod.reshape(depth, 1, n))


def _col_blocks(d):
    return [slice(c0, c0 + NORM_COL_BLOCK) for c0 in range(0, d, NORM_COL_BLOCK)]


def _rstd_from_partial(acc, d):
    return lax.rsqrt(jnp.sum(acc, axis=-1, keepdims=True) * (1.0 / d) + EPS)


def _ffn_kernel(*refs, row0, post, emit_w, n_in):
    x_ref, mod_ref, gn_ref, wg_ref, wu_ref, wd_ref, pn_ref = refs[:7]
    refs = list(refs[n_in:])
    o_ref = refs.pop(0)
    hn_ref = refs.pop(0) if post == 'mix' else None
    wg16_ref, wu16_ref, wd16_ref = (refs.pop(0), refs.pop(0), refs.pop(0)) if emit_w else (None, None, None)
    (h_ref,) = refs if post != 'mix' else (hn_ref,)
    j = pl.program_id(1)
    nj = pl.num_programs(1)
    tm, d = x_ref.shape
    rc = FFN_ROW_CHUNK
    blocks = _col_blocks(d)
    hids = [slice(f0, f0 + FFN_FF_TILE) for f0 in range(0, wg_ref.shape[1], FFN_FF_TILE)]

    def sub_tile(hid, rows, first):
        wg = wg_ref[:, hid].astype(BF16)
        wu = wu_ref[:, hid].astype(BF16)
        if emit_w:
            wg16_ref[:, hid] = wg
            wu16_ref[:, hid] = wu
        gate = jnp.dot(h_ref[rows, :], wg, preferred_element_type=F32)
        up = jnp.dot(h_ref[rows, :], wu, preferred_element_type=F32)
        a = (_silu(gate) * up).astype(BF16)
        for n0 in range(0, d, FFN_DOWN_CHUNK):
            cols = slice(n0, n0 + FFN_DOWN_CHUNK)
            wd = wd_ref[hid, cols].astype(BF16)
            if emit_w:
                wd16_ref[hid, cols] = wd
            y = jnp.dot(a, wd, preferred_element_type=F32)
            if first:
                o_ref[rows, cols] = y
            else:
                o_ref[rows, cols] += y

    @pl.when(j == 0)
    def _():
        shift = mod_ref[row0:row0 + 1, :]
        gs = gn_ref[...] * (1.0 + mod_ref[row0 + 1:row0 + 2, :])
        for r0 in range(0, tm, FFN_EDGE_ROW_BLOCK):
            r_end = r0 + FFN_EDGE_ROW_BLOCK
            for r1 in range(r0, r_end, rc):
                rows = slice(r1, r1 + rc)
                acc = jnp.zeros((rc, NORM_COL_BLOCK), F32)
                for cols in blocks:
                    xb = x_ref[rows, cols]
                    acc = acc + xb * xb
                rstd = _rstd_from_partial(acc, d)
                for cols in blocks:
                    h_ref[rows, cols] = (x_ref[rows, cols] * rstd * gs[:, cols] + shift[:, cols]).astype(BF16)
            rows = slice(r0, r_end)
            for hid in hids:
                sub_tile(hid, rows, first=hid is hids[0])

    @pl.when(jnp.logical_and(j > 0, j < nj - 1))
    def _():
        for hid in hids:
            sub_tile(hid, slice(None), first=False)

    @pl.when(j == nj - 1)
    def _():
        res_gate = MACARON * mod_ref[row0 + 2:row0 + 3, :]
        pn = pn_ref[...]
        if post == 'mix':
            shift2 = mod_ref[3:4, :]
            gs2 = pn * (1.0 + mod_ref[4:5, :])
        for r0 in range(0, tm, FFN_EDGE_ROW_BLOCK):
            for hid in hids:
                sub_tile(hid, slice(r0, r0 + FFN_EDGE_ROW_BLOCK), first=False)
            for r1 in range(r0, r0 + FFN_EDGE_ROW_BLOCK, rc):
                rows = slice(r1, r1 + rc)
                acc = jnp.zeros((rc, NORM_COL_BLOCK), F32)
                for cols in blocks:
                    xn = x_ref[rows, cols] + res_gate[:, cols] * o_ref[rows, cols]
                    o_ref[rows, cols] = xn
                    if post is not None:
                        acc = acc + xn * xn
                if post is not None:
                    rstd = _rstd_from_partial(acc, d)
                    for cols in blocks:
                        if post == 'final':
                            o_ref[rows, cols] = o_ref[rows, cols] * rstd * pn[:, cols]
                        else:
                            hn_ref[rows, cols] = (o_ref[rows, cols] * rstd * gs2[:, cols]
                                                  + shift2[:, cols]).astype(BF16)


def _ffn(x, mod, g_norm, weights, post_norm, *, row0, post, tiles=None, li=None, hn_buf=None):
    nb, t, d = x.shape
    tm = min(FFN_TOKEN_TILE, t)
    tpb = t // tm
    emit_w = li is not None
    inplace = tiles is not None
    t0, nt = tiles if inplace else (0, nb * tpb)
    w_gate, w_up, w_down = weights
    d_ff = w_gate.shape[-1]
    tf = FFN_FF_TILE if emit_w else FFN_FF_TILE_BF16

    def rows(n, j):
        return ((t0 + n) // tpb, (t0 + n) % tpb, 0)

    if emit_w:
        w_specs = [pl.BlockSpec((None, d, tf), lambda n, j: (li, 0, j)),
                   pl.BlockSpec((None, d, tf), lambda n, j: (li, 0, j)),
                   pl.BlockSpec((None, tf, d), lambda n, j: (li, j, 0))]
    else:
        w_specs = [pl.BlockSpec((d, tf), lambda n, j: (0, j)),
                   pl.BlockSpec((d, tf), lambda n, j: (0, j)),
                   pl.BlockSpec((tf, d), lambda n, j: (j, 0))]
    in_specs = [pl.BlockSpec((None, tm, d), rows),
                pl.BlockSpec((None, N_MOD, d), lambda n, j: ((t0 + n) // tpb, 0, 0)),
                pl.BlockSpec((1, d), lambda n, j: (0, 0)),
                *w_specs,
                pl.BlockSpec((1, d), lambda n, j: (0, 0))]
    args = [x, mod, g_norm.reshape(1, d), w_gate, w_up, w_down, post_norm.reshape(1, d)]
    out_specs = [pl.BlockSpec((None, tm, d), rows)]
    out_shape = [jax.ShapeDtypeStruct((nb, t, d), F32)]
    aliases = {0: 0} if inplace else {}
    if post == 'mix':
        out_specs.append(pl.BlockSpec((None, tm, d), rows))
        out_shape.append(jax.ShapeDtypeStruct((nb, t, d), BF16))
        if inplace:
            in_specs.append(pl.BlockSpec(memory_space=pl.ANY))
            args.append(hn_buf)
            aliases[len(args) - 1] = 1
    if emit_w:
        out_specs += [pl.BlockSpec((d, tf), lambda n, j: (0, j)),
                      pl.BlockSpec((d, tf), lambda n, j: (0, j)),
                      pl.BlockSpec((tf, d), lambda n, j: (j, 0))]
        out_shape += [jax.ShapeDtypeStruct((d, d_ff), BF16), jax.ShapeDtypeStruct((d, d_ff), BF16),
                      jax.ShapeDtypeStruct((d_ff, d), BF16)]
        assert nt == 1, "the bf16 weight casts are emitted once, by a single-tile call"
    out = pl.pallas_call(
        functools.partial(_ffn_kernel, row0=row0, post=post, emit_w=emit_w, n_in=len(args)),
        grid=(nt, d_ff // tf),
        in_specs=in_specs,
        out_specs=out_specs,
        out_shape=out_shape,
        input_output_aliases=aliases,
        scratch_shapes=[] if post == 'mix' else [pltpu.VMEM((tm, d), BF16)],
        compiler_params=_params(2),
        name="ffn",
    )(*args)
    hn = out[1] if post == 'mix' else None
    w16 = tuple(out[-3:]) if emit_w else None
    return out[0], hn, w16


def _ffn_stream(x, mod, g_norm, weights, post_norm, *, li, row0, post, hn_buf=None):
    nb, t, d = x.shape
    n_tiles = nb * (t // min(FFN_TOKEN_TILE, t))
    if post == 'mix' and hn_buf is None:
        hn_buf = jnp.zeros((nb, t, d), BF16)
    x, hn, w16 = _ffn(x, mod, g_norm, weights, post_norm, row0=row0, post=post, tiles=(0, 1), li=li, hn_buf=hn_buf)
    if n_tiles > 1:
        x, hn, _ = _ffn(x, mod, g_norm, w16, post_norm, row0=row0, post=post, tiles=(1, n_tiles - 1), hn_buf=hn)
    return x, hn


def _ctx_kv_kernel(h_ref, w_ref, o_ref):
    o_ref[...] = jnp.dot(h_ref[...], w_ref[...].astype(BF16), preferred_element_type=F32)


def _ctx_kv(hc, w_in, e):
    t, d = hc.shape
    tn = CTX_KV_COL_TILE
    col0 = RET_W // tn
    return pl.pallas_call(
        _ctx_kv_kernel,
        grid=(2 * RET_W // tn,),
        in_specs=[
            pl.BlockSpec((t, d), lambda j: (0, 0)),
            pl.BlockSpec((None, d, tn), lambda j: (e, 0, col0 + j)),
        ],
        out_specs=pl.BlockSpec((t, tn), lambda j: (0, j)),
        out_shape=jax.ShapeDtypeStruct((t, 2 * RET_W), F32),
        compiler_params=_params(1),
        name="ctx_kv",
    )(hc, w_in)


def _ret_kernel(lgf_ref, lgb_ref, h_ref, wq_ref, wk_ref, wv_ref, wg_ref, kc_ref, vc_ref,
                cos_ref, sin_ref, o_ref, w_s, p_s):
    hd = RET_HEAD_DIM
    for t in range(RET_HEADS_PER_STEP):
        cols = slice(t * hd, (t + 1) * hd)
        w_s[t, :, 0 * hd:1 * hd] = wq_ref[:, cols].astype(BF16)
        w_s[t, :, 1 * hd:2 * hd] = wk_ref[:, cols].astype(BF16)
        w_s[t, :, 2 * hd:3 * hd] = wv_ref[:, cols].astype(BF16)
        w_s[t, :, 3 * hd:4 * hd] = wg_ref[:, cols].astype(BF16)
        p_s[t] = jnp.dot(h_ref[...], w_s[t], preferred_element_type=F32)
    for t in range(RET_HEADS_PER_STEP):
        cols = slice(t * hd, (t + 1) * hd)
        _ret_head(lgf_ref[t, 0:1, 0:1], lgb_ref[t, 0:1, 0:1], p_s.at[t], kc_ref[:, cols], vc_ref[:, cols],
                  cos_ref[...], sin_ref[...], o_ref, cols)


def _ret_head(lgf, lgb, p_s, kc, vcx, cos2, sin2, o_ref, out_cols):
    hd = RET_HEAD_DIM
    l = p_s.shape[0]
    lc = kc.shape[0]
    c = RET_CHUNK
    n_chunks = l // c

    q = p_s[:, 0 * hd:1 * hd]
    p_s[:, 0 * hd:1 * hd] = q * cos2 + pltpu.roll(q, hd // 2, axis=1) * sin2
    k = p_s[:, 1 * hd:2 * hd] * K_SCALE
    p_s[:, 1 * hd:2 * hd] = k * cos2 + pltpu.roll(k, hd // 2, axis=1) * sin2

    pos_c = lax.broadcasted_iota(jnp.int32, (1, lc), 1).astype(F32)
    kct = (kc * K_SCALE).T
    s_fwd = _bdot(kct * jnp.exp(lgf * (lc - 1.0 - pos_c)), vcx)
    s_bwd = _bdot(kct * jnp.exp(lgb * pos_c), vcx)

    pos = lax.broadcasted_iota(jnp.int32, (1, c), 1).astype(F32)
    wf_row = jnp.exp(lgf * (c - 1.0 - pos))
    wb_row = jnp.exp(lgb * pos)
    kv_f, kv_b = [], []
    for n in range(n_chunks):
        rows = slice(n * c, (n + 1) * c)
        kt = p_s[rows, 1 * hd:2 * hd].T
        kw = jnp.concatenate([kt * wf_row, kt * wb_row], axis=0)
        kv = _bdot(kw, p_s[rows, 2 * hd:3 * hd])
        kv_f.append(kv[:hd])
        kv_b.append(kv[hd:])

    gfc = jnp.exp(lgf * float(c))
    gbc = jnp.exp(lgb * float(c))
    st_f = [s_fwd]
    for n in range(n_chunks - 1):
        st_f.append(st_f[-1] * gfc + kv_f[n])
    st_b = [s_bwd]
    for n in range(n_chunks - 1, 0, -1):
        st_b.append(st_b[-1] * gbc + kv_b[n])
    st_b = st_b[::-1]

    ri = lax.broadcasted_iota(jnp.int32, (c, c), 0)
    ci = lax.broadcasted_iota(jnp.int32, (c, c), 1)
    diff = (ri - ci).astype(F32)
    decay = jnp.where(diff > 0, jnp.exp(lgf * jnp.maximum(diff, 0.0)),
                      jnp.where(diff < 0, jnp.exp(lgb * jnp.maximum(-diff, 0.0)), 2.0))
    rpos = lax.broadcasted_iota(jnp.int32, (c, 1), 0).astype(F32)
    qf_col = jnp.exp(lgf * (rpos + 1.0))
    qb_col = jnp.exp(lgb * (c - rpos))

    for n in range(n_chunks):
        rows = slice(n * c, (n + 1) * c)
        qn = p_s[rows, 0 * hd:1 * hd]
        kn = p_s[rows, 1 * hd:2 * hd]
        vn = p_s[rows, 2 * hd:3 * hd]
        gn = p_s[rows, 3 * hd:4 * hd]
        scores = lax.dot_general(qn.astype(BF16), kn.astype(BF16), (((1,), (1,)), ((), ())),
                                 preferred_element_type=F32)
        o = _bdot(scores * decay, vn)
        qd = jnp.concatenate([qn * qf_col, qn * qb_col], axis=1)
        st = jnp.concatenate([st_f[n], st_b[n]], axis=0)
        o = o + _bdot(qd, st)
        mu = jnp.mean(o, axis=-1, keepdims=True)
        oc = o - mu
        var = jnp.mean(oc * oc, axis=-1, keepdims=True)
        o_ref[rows, out_cols] = (oc * lax.rsqrt(var + GN_EPS) * _silu(gn)).astype(BF16)


def _retention(h, w_in, e, kv_ctx, lg_f, lg_b, cos2, sin2):
    nb, l, d = h.shape
    lc = kv_ctx.shape[1]
    hd = RET_HEAD_DIM
    g = RET_HEADS_PER_STEP
    ng = RET_HEADS // g
    lgf = jnp.broadcast_to(lg_f.astype(F32)[:, None, None], (RET_HEADS, 8, 128))
    lgb = jnp.broadcast_to(lg_b.astype(F32)[:, None, None], (RET_HEADS, 8, 128))

    def wspec(part):
        return pl.BlockSpec((None, d, g * hd), lambda b, n: (e, 0, part * ng + n))

    return pl.pallas_call(
        _ret_kernel,
        grid=(nb, ng),
        in_specs=[
            pl.BlockSpec((g, 8, 128), lambda b, n: (n, 0, 0)),
            pl.BlockSpec((g, 8, 128), lambda b, n: (n, 0, 0)),
            pl.BlockSpec((None, l, d), lambda b, n: (b, 0, 0)),
            wspec(0), wspec(1), wspec(2), wspec(3),
            pl.BlockSpec((None, lc, g * hd), lambda b, n: (b, 0, n)),
            pl.BlockSpec((None, lc, g * hd), lambda b, n: (b, 0, ng + n)),
            pl.BlockSpec((l, hd), lambda b, n: (0, 0)),
            pl.BlockSpec((l, hd), lambda b, n: (0, 0)),
        ],
        out_specs=pl.BlockSpec((None, l, g * hd), lambda b, n: (b, 0, n)),
        out_shape=jax.ShapeDtypeStruct((nb, l, RET_W), BF16),
        scratch_shapes=[pltpu.VMEM((g, d, 4 * hd), BF16), pltpu.VMEM((g, l, 4 * hd), F32)],
        compiler_params=_params(2),
        name="retention",
    )(lgf, lgb, h, w_in, w_in, w_in, w_in, kv_ctx, kv_ctx, cos2, sin2)


CONV_COL_TILE = 256


def _conv_kernel(h_ref, wb_ref, wc_ref, wu_ref, wconv_ref, wo_ref, o_ref, wo16_ref, w_s, p_s):
    tc = CONV_COL_TILE
    wo16_ref[...] = wo_ref[...].astype(BF16)
    l = h_ref.shape[0]
    w_s[:, 0 * tc:1 * tc] = wb_ref[...].astype(BF16)
    w_s[:, 1 * tc:2 * tc] = wc_ref[...].astype(BF16)
    w_s[:, 2 * tc:3 * tc] = wu_ref[...].astype(BF16)
    p_s[...] = jnp.dot(h_ref[...], w_s[...], preferred_element_type=F32)
    z = p_s[:, 1 * tc:2 * tc] * p_s[:, 2 * tc:3 * tc]
    t = lax.broadcasted_iota(jnp.int32, (l, 1), 0)
    z_prev = jnp.where(t == 0, 0.0, pltpu.roll(z, 1, axis=0))
    z_next = jnp.where(t == l - 1, 0.0, pltpu.roll(z, l - 1, axis=0))
    conv = z_prev * wconv_ref[0:1, :] + z * wconv_ref[1:2, :] + z_next * wconv_ref[2:3, :]
    o_ref[...] = (p_s[:, 0 * tc:1 * tc] * conv).astype(BF16)


def _gated_conv(h, w_in, w_conv, w_out, e):
    nb, l, d = h.shape
    tc = CONV_COL_TILE
    nt = CONV_W // tc
    col0 = 4 * RET_W // tc
    slab = d // (nb * nt)
    assert slab * nb * nt == d and slab % 16 == 0

    def wspec(part):
        return pl.BlockSpec((None, d, tc), lambda b, n: (e, 0, col0 + part * nt + n))

    return pl.pallas_call(
        _conv_kernel,
        grid=(nb, nt),
        in_specs=[
            pl.BlockSpec((None, l, d), lambda b, n: (b, 0, 0)),
            wspec(0), wspec(1), wspec(2),
            pl.BlockSpec((None, 3, tc), lambda b, n: (e, 0, n)),
            pl.BlockSpec((None, slab, d), lambda b, n: (e, b * nt + n, 0)),
        ],
        out_specs=[pl.BlockSpec((None, l, tc), lambda b, n: (b, 0, n)),
                   pl.BlockSpec((slab, d), lambda b, n: (b * nt + n, 0))],
        out_shape=[jax.ShapeDtypeStruct((nb, l, CONV_W), BF16), jax.ShapeDtypeStruct((d, d), BF16)],
        scratch_shapes=[pltpu.VMEM((d, 3 * tc), BF16), pltpu.VMEM((l, 3 * tc), F32)],
        compiler_params=_params(2),
        name="gated_conv",
    )(h, w_in, w_in, w_in, w_conv, w_out)


def _outproj_kernel(ar_ref, ac_ref, w_ref, x_ref, mod_ref, o_ref):
    d = o_ref.shape[1]
    for n0 in range(0, d, OUT_COL_CHUNK):
        cols = slice(n0, n0 + OUT_COL_CHUNK)
        y = jnp.dot(ar_ref[...], w_ref[:RET_W, cols], preferred_element_type=F32)
        y = y + jnp.dot(ac_ref[...], w_ref[RET_W:, cols], preferred_element_type=F32)
        o_ref[:, cols] = x_ref[:, cols] + mod_ref[5:6, cols] * y


def _outproj(a_ret, a_conv, w_out16, x, mod):
    nb, l, d = x.shape
    tm = OUT_ROW_TILE
    return pl.pallas_call(
        _outproj_kernel,
        grid=(nb, l // tm),
        in_specs=[
            pl.BlockSpec((None, tm, RET_W), lambda b, i: (b, i, 0)),
            pl.BlockSpec((None, tm, CONV_W), lambda b, i: (b, i, 0)),
            pl.BlockSpec((d, d), lambda b, i: (0, 0)),
            pl.BlockSpec((None, tm, d), lambda b, i: (b, i, 0)),
            pl.BlockSpec((None, N_MOD, d), lambda b, i: (b, 0, 0)),
        ],
        out_specs=pl.BlockSpec((None, tm, d), lambda b, i: (b, i, 0)),
        out_shape=jax.ShapeDtypeStruct((nb, l, d), F32),
        compiler_params=_params(2),
        name="outproj",
    )(a_ret, a_conv, w_out16, x, mod)


POOL_ROW_TILE = 512
POOL_ROW_CHUNK = 256
POOL_HALO = 16


def _pool_kernel(h_ref, w_ref, ps_ref, x_ref, mod_ref, o_ref, hp_s):
    l, d = h_ref.shape
    tr = x_ref.shape[0]
    r, halo, gw = POOL_ROW_CHUNK, POOL_HALO, POOL_GROUP_W
    i = pl.program_id(1)
    ni = pl.num_programs(1)
    r0 = pl.multiple_of(i * tr, tr)

    hp_s[halo:halo + tr, :] = h_ref[pl.ds(r0, tr), :]

    @pl.when(i == 0)
    def _():
        hp_s[0:halo, :] = jnp.zeros((halo, d), BF16)

    @pl.when(i > 0)
    def _():
        hp_s[0:halo, :] = h_ref[pl.ds(pl.multiple_of(r0 - halo, halo), halo), :]

    @pl.when(i == ni - 1)
    def _():
        hp_s[halo + tr:, :] = jnp.zeros((halo, d), BF16)

    @pl.when(i < ni - 1)
    def _():
        hp_s[halo + tr:, :] = h_ref[pl.ds(pl.multiple_of(r0 + tr, halo), halo), :]

    ri = lax.broadcasted_iota(jnp.int32, (r, r + 2 * halo), 0)
    ci = lax.broadcasted_iota(jnp.int32, (r, r + 2 * halo), 1)
    off = ci - halo - ri
    for gi, w in enumerate(POOL_WINDOWS):
        lo, hi = w // 2, w - w // 2
        gcols = slice(gi * gw, (gi + 1) * gw)
        band = jnp.where(off >= -lo, jnp.where(off < hi, 1.0, 0.0), 0.0).astype(BF16)
        parts = []
        for n in range(tr // r):
            t = r0 + n * r + lax.broadcasted_iota(jnp.int32, (r, 1), 0)
            cnt = (jnp.minimum(t + hi, l) - jnp.maximum(t - lo, 0)).astype(F32)
            win = jnp.dot(band, hp_s[n * r:(n + 1) * r + 2 * halo, gcols], preferred_element_type=F32)
            hc = hp_s[n * r + halo:(n + 1) * r + halo, gcols].astype(F32)
            parts.append((win / cnt - hc).astype(BF16))
        pooled = jnp.concatenate(parts, axis=0)
        y = jnp.dot(pooled, w_ref[gi].astype(BF16), preferred_element_type=F32) * ps_ref[:, gcols]
        o_ref[:, gcols] = x_ref[:, gcols] + mod_ref[5:6, gcols] * y


def _pool(h, pool_w, pool_scale, o, x, mod):
    nb, l, d = x.shape
    tr = POOL_ROW_TILE
    ng = len(POOL_WINDOWS)
    gw = POOL_GROUP_W
    assert max(POOL_WINDOWS) // 2 <= POOL_HALO and tr % POOL_ROW_CHUNK == 0 and l % tr == 0
    return pl.pallas_call(
        _pool_kernel,
        grid=(nb, l // tr),
        in_specs=[
            pl.BlockSpec((None, l, d), lambda b, i: (b, 0, 0)),
            pl.BlockSpec((None, ng, gw, gw), lambda b, i: (o, 0, 0, 0)),
            pl.BlockSpec((None, 1, d), lambda b, i: (o, 0, 0)),
            pl.BlockSpec((None, tr, d), lambda b, i: (b, i, 0)),
            pl.BlockSpec((None, N_MOD, d), lambda b, i: (b, 0, 0)),
        ],
        out_specs=pl.BlockSpec((None, tr, d), lambda b, i: (b, i, 0)),
        out_shape=jax.ShapeDtypeStruct((nb, l, d), F32),
        scratch_shapes=[pltpu.VMEM((tr + 2 * POOL_HALO, d), BF16)],
        compiler_params=_params(2),
        name="pool",
    )(h, pool_w, pool_scale.reshape(-1, 1, d), x, mod)


def _rope_tables(l):
    quarter = RET_HEAD_DIM // 4
    inv = ROPE_BASE ** (-jnp.arange(quarter, dtype=F32) / quarter)
    rows = (jnp.arange(l) // GRID_W).astype(F32)
    cols = (jnp.arange(l) % GRID_W).astype(F32)
    ang = jnp.concatenate([rows[:, None] * inv, cols[:, None] * inv], axis=-1)
    cos, sin = jnp.cos(ang), jnp.sin(ang)
    return jnp.concatenate([cos, cos], axis=-1), jnp.concatenate([-sin, sin], axis=-1)


def kernel(x, c, ctx, c_ctx, w_mod, b_mod, norm_ffn1, norm_mix, norm_ffn2, ffn1_w_gate, ffn1_w_up, ffn1_w_down,
           ffn2_w_gate, ffn2_w_up, ffn2_w_down, mix_w_in, mix_w_conv, mix_w_out, ret_decay_fwd, ret_decay_bwd,
           pool_w, pool_scale, final_norm):
    b, l, d = x.shape
    depth = w_mod.shape[0]
    lc = ctx.shape[1]
    assert b + 1 <= 8 and d == D_MODEL
    last_even = ((depth - 1) // 2) * 2

    cond8 = jnp.zeros((8, d), F32).at[:b].set(c).at[b].set(c_ctx)
    mod_all = _adaln(cond8, w_mod, b_mod).reshape(depth, 8, N_MOD, d)
    cos2, sin2 = _rope_tables(l)

    xc = ctx.reshape(1, b * lc, d)
    h = None
    for li in range(depth):
        mod = mod_all[li, :b]
        mod_ctx = mod_all[li, b:b + 1]
        ctx_needed = li <= last_even
        ctx_full = li < last_even
        ffn1_w = (ffn1_w_gate, ffn1_w_up, ffn1_w_down)
        if ctx_needed:
            xc, hc, w16 = _ffn(xc, mod_ctx, norm_ffn1[li], ffn1_w, norm_mix[li], li=li, row0=0, post='mix')
            x, h, _ = _ffn(x, mod, norm_ffn1[li], w16, norm_mix[li], row0=0, post='mix')
        else:
            x, h = _ffn_stream(x, mod, norm_ffn1[li], ffn1_w, norm_mix[li], li=li, row0=0, post='mix', hn_buf=h)
        if li % 2 == 0:
            e = li // 2
            assert not ctx_full, "a context stream that is read again by a later layer is not implemented"
            lg_f = jax.nn.log_sigmoid(ret_decay_fwd[e])
            lg_b = jax.nn.log_sigmoid(ret_decay_bwd[e])
            kv_ctx = _ctx_kv(hc[0], mix_w_in, e).reshape(b, lc, 2 * RET_W)
            a_ret = _retention(h, mix_w_in, e, kv_ctx, lg_f, lg_b, cos2, sin2)
            a_conv, w_out16 = _gated_conv(h, mix_w_in, mix_w_conv, mix_w_out, e)
            x = _outproj(a_ret, a_conv, w_out16, x, mod)
        else:
            x = _pool(h, pool_w, pool_scale, li // 2, x, mod)
        x, _ = _ffn_stream(x, mod, norm_ffn2[li], (ffn2_w_gate, ffn2_w_up, ffn2_w_down), final_norm,
                           li=li, row0=6, post=('final' if li == depth - 1 else None))
    return x
```
